```python
import jax, jax.numpy as jnp
from jax import lax
import numpy as np

D_MODEL = 2048
BATCH = 2
SEQ = 8192
DEPTH = 2

HEAD_DIM = 128
N_HEADS_A = 16
N_HEADS_B = 16
MOBA_BLOCK = 256
MOBA_TOPK = 3
MOBA_QCHUNK = 32
SB_QBLOCK = 128
D_FF_DENSE = 5632
N_EXPERTS = 8
MOE_TOPK = 2
D_FF_EXPERT = 7168
N_A_LAYERS = DEPTH // 2
N_B_LAYERS = DEPTH - N_A_LAYERS
N_DENSE_LAYERS = (DEPTH + 1) // 2
N_MOE_LAYERS = DEPTH // 2
RMS_EPS = 1e-6
NEG_INF = -1e30

kernel_name = "yoco_moba_stickbreak_moe_block"


def rms_norm(x, g):
    x32 = x.astype(jnp.float32)
    y = x32 * lax.rsqrt(jnp.mean(x32 * x32, axis=-1, keepdims=True) + RMS_EPS)
    return y.astype(x.dtype) * g


def split_heads(t, n_heads):
    b, s, _ = t.shape
    return t.reshape(b, s, n_heads, HEAD_DIM).transpose(0, 2, 1, 3)


def merge_heads(t):
    b, h, s, d = t.shape
    return t.transpose(0, 2, 1, 3).reshape(b, s, h * d)


def alibi_slopes(n_heads):
    return jnp.asarray(2.0 ** (-8.0 * np.arange(1, n_heads + 1) / n_heads), dtype=jnp.float32)


def moba_attention(q, k, v):
    b, h, s, dh = q.shape
    nb = -(-s // MOBA_BLOCK)
    sp = nb * MOBA_BLOCK
    pad = ((0, 0), (0, 0), (0, sp - s), (0, 0))
    q, k, v = jnp.pad(q, pad), jnp.pad(k, pad), jnp.pad(v, pad)
    k_sel = min(MOBA_TOPK, nb)
    scale = dh ** -0.5
    kb = k.reshape(b, h, nb, MOBA_BLOCK, dh)
    vb = v.reshape(b, h, nb, MOBA_BLOCK, dh)
    kmean = jnp.mean(kb.astype(jnp.float32), axis=3)
    qblk = jnp.arange(sp) // MOBA_BLOCK
    gate = jnp.einsum('bhsd,bhnd->bhsn', q.astype(jnp.float32), kmean)
    past = jnp.arange(nb)[None, :] < qblk[:, None]
    gate = jnp.where(past[None, None], gate, NEG_INF)
    _, sel = lax.top_k(gate, k_sel)
    slopes = alibi_slopes(h)
    bi = jnp.arange(b)[:, None, None, None]
    hi = jnp.arange(h)[None, :, None, None]
    offs = jnp.arange(MOBA_BLOCK)
    C = MOBA_QCHUNK

    def chunk(c):
        t0 = c * C
        qc = lax.dynamic_slice_in_dim(q, t0, C, axis=2)
        selc = lax.dynamic_slice_in_dim(sel, t0, C, axis=2)
        tq = t0 + jnp.arange(C)
        valid = jnp.arange(k_sel)[None, :] < (tq // MOBA_BLOCK)[:, None]
        ks = kb[bi, hi, selc]
        vs = vb[bi, hi, selc]
        own = t0 // MOBA_BLOCK
        ko = lax.dynamic_index_in_dim(kb, own, axis=2, keepdims=False)
        vo = lax.dynamic_index_in_dim(vb, own, axis=2, keepdims=False)
        s_sel = jnp.einsum('bhcd,bhckld->bhckl', qc, ks).astype(jnp.float32) * scale
        s_own = jnp.einsum('bhcd,bhld->bhcl', qc, ko).astype(jnp.float32) * scale
        kpos_sel = selc[..., None] * MOBA_BLOCK + offs
        dist_sel = (tq[None, None, :, None, None] - kpos_sel).astype(jnp.float32)
        s_sel = s_sel - slopes[None, :, None, None, None] * dist_sel
        s_sel = jnp.where(valid[None, None, :, :, None], s_sel, NEG_INF)
        dist_own = tq[:, None] - (own * MOBA_BLOCK + offs)[None, :]
        s_own = jnp.where((dist_own >= 0)[None, None],
                          s_own - slopes[None, :, None, None] * dist_own.astype(jnp.float32)[None, None],
                          NEG_INF)
        scores = jnp.concatenate([s_sel.reshape(b, h, C, k_sel * MOBA_BLOCK), s_own], axis=-1)
        p = jax.nn.softmax(scores, axis=-1).astype(v.dtype)
        p_sel = p[..., :k_sel * MOBA_BLOCK].reshape(b, h, C, k_sel, MOBA_BLOCK)
        p_own = p[..., k_sel * MOBA_BLOCK:]
        return (jnp.einsum('bhckl,bhckld->bhcd', p_sel, vs)
                + jnp.einsum('bhcl,bhld->bhcd', p_own, vo))

    outs = lax.map(chunk, jnp.arange(sp // C))
    out = outs.transpose(1, 2, 0, 3, 4).reshape(b, h, sp, dh)
    return out[:, :, :s]


def stick_breaking_attention(q, k, v):
    b, h, s, dh = q.shape
    scale = dh ** -0.5
    kpos = jnp.arange(s)

    def block(i):
        t0 = i * SB_QBLOCK
        qb = lax.dynamic_slice_in_dim(q, t0, SB_QBLOCK, axis=2)
        z = jnp.einsum('bhqd,bhkd->bhqk', qb, k).astype(jnp.float32) * scale
        tq = t0 + jnp.arange(SB_QBLOCK)
        causal = (kpos[None, :] < tq[:, None])[None, None]
        log_keep = jnp.where(causal, jax.nn.log_sigmoid(-z), 0.0)
        suffix = lax.cumsum(log_keep, axis=3, reverse=True) - log_keep
        w = jnp.where(causal, jnp.exp(jax.nn.log_sigmoid(z) + suffix), 0.0)
        return jnp.einsum('bhqk,bhkd->bhqd', w.astype(v.dtype), v)

    outs = lax.map(block, jnp.arange(s // SB_QBLOCK))
    return outs.transpose(1, 2, 0, 3, 4).reshape(b, h, s, dh)


def swiglu(x, w1, w3, w2):
    return (jax.nn.silu(x @ w1) * (x @ w3)) @ w2


def moe_swiglu(x, w_router, w1, w3, w2):
    b, s, d = x.shape
    xf = x.reshape(-1, d)
    logits = (xf @ w_router).astype(jnp.float32)
    top_val, top_idx = lax.top_k(logits, MOE_TOPK)
    top_w = jax.nn.softmax(top_val, axis=-1)
    gates = jnp.sum(jax.nn.one_hot(top_idx, N_EXPERTS, dtype=jnp.float32) * top_w[..., None], axis=1)
    gates = gates.astype(x.dtype)
    out = jnp.zeros_like(xf)
    for e in range(N_EXPERTS):
        out = out + gates[:, e:e + 1] * swiglu(xf, w1[e], w3[e], w2[e])
    return out.reshape(b, s, d)


def setup_inputs(seed: int = 0) -> dict:
    key = jax.random.key(seed)
    ks = jax.random.split(key, 20)
    D = D_MODEL
    wa = N_HEADS_A * HEAD_DIM
    wb = N_HEADS_B * HEAD_DIM

    def w(k, shape, fan_in):
        return jax.random.normal(k, shape, dtype=jnp.float32) * (fan_in ** -0.5)

    def gain(k, shape):
        return 1.0 + 0.02 * jax.random.normal(k, shape, dtype=jnp.float32)

    return {
        "x": jax.random.normal(ks[0], (BATCH, SEQ, D), dtype=jnp.float32),
        "attn_norm_g": gain(ks[1], (DEPTH, D)),
        "ffn_norm_g": gain(ks[2], (DEPTH, D)),
        "a_w_qkv": w(ks[3], (N_A_LAYERS, D, 3 * wa), D),
        "a_w_o": w(ks[4], (N_A_LAYERS, wa, D), wa),
        "kv_norm_g": gain(ks[5], (D,)),
        "b_w_kv": w(ks[6], (D, 2 * wb), D),
        "b_w_q": w(ks[7], (N_B_LAYERS, D, wb), D),
        "b_w_o": w(ks[8], (N_B_LAYERS, wb, D), wb),
        "dense_w1": w(ks[9], (N_DENSE_LAYERS, D, D_FF_DENSE), D),
        "dense_w3": w(ks[10], (N_DENSE_LAYERS, D, D_FF_DENSE), D),
        "dense_w2": w(ks[11], (N_DENSE_LAYERS, D_FF_DENSE, D), D_FF_DENSE),
        "moe_w_router": w(ks[12], (N_MOE_LAYERS, D, N_EXPERTS), D),
        "moe_w1": w(ks[13], (N_MOE_LAYERS, N_EXPERTS, D, D_FF_EXPERT), D),
        "moe_w3": w(ks[14], (N_MOE_LAYERS, N_EXPERTS, D, D_FF_EXPERT), D),
        "moe_w2": w(ks[15], (N_MOE_LAYERS, N_EXPERTS, D_FF_EXPERT, D), D_FF_EXPERT),
        "final_norm_g": gain(ks[16], (D,)),
    }


def reference(x, attn_norm_g, ffn_norm_g, a_w_qkv, a_w_o, kv_norm_g, b_w_kv, b_w_q, b_w_o,
              dense_w1, dense_w3, dense_w2, moe_w_router, moe_w1, moe_w3, moe_w2, final_norm_g):
    h = x
    k_sh = None
    v_sh = None
    for layer in range(DEPTH):
        if layer == N_A_LAYERS:
            kv = rms_norm(h, kv_norm_g) @ b_w_kv
            k_flat, v_flat = jnp.split(kv, 2, axis=-1)
            k_sh = split_heads(k_flat, N_HEADS_B)
            v_sh = split_heads(v_flat, N_HEADS_B)
        xn = rms_norm(h, attn_norm_g[layer])
        if layer < N_A_LAYERS:
            qkv = xn @ a_w_qkv[layer]
            q, k, v = jnp.split(qkv, 3, axis=-1)
            mix = moba_attention(split_heads(q, N_HEADS_A), split_heads(k, N_HEADS_A),
                                 split_heads(v, N_HEADS_A))
            h = h + merge_heads(mix) @ a_w_o[layer]
        else:
            j = layer - N_A_LAYERS
            q = split_heads(xn @ b_w_q[j], N_HEADS_B)
            mix = stick_breaking_attention(q, k_sh, v_sh)
            h = h + merge_heads(mix) @ b_w_o[j]
        xn = rms_norm(h, ffn_norm_g[layer])
        if layer % 2 == 0:
            i = layer // 2
            h = h + swiglu(xn, dense_w1[i], dense_w3[i], dense_w2[i])
        else:
            i = layer // 2
            h = h + moe_swiglu(xn, moe_w_router[i], moe_w1[i], moe_w3[i], moe_w2[i])
    return rms_norm(h, final_norm_g)
```

```python
import functools

import jax
import jax.numpy as jnp
from jax import lax
from jax.experimental import pallas as pl
from jax.experimental.pallas import tpu as pltpu

HEAD_DIM = 128
MOBA_BLOCK = 256
MOBA_TOPK = 3
MOE_TOPK = 2
RMS_EPS = 1e-6
NEG_INF = -1e30
LANES = 128
VMEM_CAP = 60 * 1024 * 1024
F32_EXP_ZERO_BELOW = -104.0

F32 = jnp.float32
BF16 = jnp.bfloat16


def _tile(dim, pref):
    t = min(dim, pref)
    while dim % t:
        t //= 2
    return t


def _params(sem, vmem_bytes):
    return pltpu.CompilerParams(dimension_semantics=sem,
                                vmem_limit_bytes=int(min(VMEM_CAP, max(vmem_bytes, 16 * 1024 * 1024))))


def _rmsnorm(x, g):
    return x * lax.rsqrt(jnp.mean(x * x, axis=-1, keepdims=True) + RMS_EPS) * g


def _norm_matmul_kernel(x_ref, g_ref, w_ref, o_ref, xn_ref):
    @pl.when(pl.program_id(1) == 0)
    def _():
        xn_ref[...] = _rmsnorm(x_ref[...], g_ref[...]).astype(BF16)

    acc = jnp.dot(xn_ref[...], w_ref[...], preferred_element_type=F32)
    for c in range(o_ref.shape[0]):
        o_ref[c] = acc[:, c * HEAD_DIM:(c + 1) * HEAD_DIM].astype(o_ref.dtype)


def norm_matmul_heads(x, g, w):
    n, d = x.shape
    dout = w.shape[1]
    tm, tn = _tile(n, 1024), _tile(dout, 512)
    vmem = 2 * tm * d * 4 + tm * d * 2 + 2 * d * tn * 2 + 2 * tm * tn * 2 + 2 * tm * tn * 4
    return pl.pallas_call(
        _norm_matmul_kernel,
        grid=(n // tm, dout // tn),
        in_specs=[pl.BlockSpec((tm, d), lambda i, j: (i, 0)),
                  pl.BlockSpec((1, d), lambda i, j: (0, 0)),
                  pl.BlockSpec((d, tn), lambda i, j: (0, j))],
        out_specs=pl.BlockSpec((tn // HEAD_DIM, tm, HEAD_DIM), lambda i, j: (j, i, 0)),
        out_shape=jax.ShapeDtypeStruct((dout // HEAD_DIM, n, HEAD_DIM), BF16),
        scratch_shapes=[pltpu.VMEM((tm, d), BF16)],
        compiler_params=_params(("arbitrary", "arbitrary"), vmem + (8 << 20)),
        name="norm_matmul",
    )(x, g.reshape(1, d), w)


def _matmul_res_kernel(a_ref, w_ref, r_ref, o_ref):
    o_ref[...] = r_ref[...] + jnp.dot(a_ref[...], w_ref[...], preferred_element_type=F32)


def matmul_res(a, w, res):
    n, k = a.shape
    dout = w.shape[1]
    tm, tn = _tile(n, 1024), _tile(dout, 512)
    vmem = 2 * (tm * k * 2 + k * tn * 2 + 2 * tm * tn * 4) + tm * tn * 4
    return pl.pallas_call(
        _matmul_res_kernel,
        grid=(n // tm, dout // tn),
        in_specs=[pl.BlockSpec((tm, k), lambda i, j: (i, 0)),
                  pl.BlockSpec((k, tn), lambda i, j: (0, j)),
                  pl.BlockSpec((tm, tn), lambda i, j: (i, j))],
        out_specs=pl.BlockSpec((tm, tn), lambda i, j: (i, j)),
        out_shape=jax.ShapeDtypeStruct((n, dout), F32),
        compiler_params=_params(("arbitrary", "arbitrary"), vmem + (8 << 20)),
        name="matmul_res",
    )(a, w, res)


def _swiglu_act(xn, w1, w3):
    gate = jnp.dot(xn, w1, preferred_element_type=F32)
    up = jnp.dot(xn, w3, preferred_element_type=F32)
    return (gate / (1.0 + jnp.exp(-gate)) * up).astype(BF16)


def _dense_ffn_kernel(x_ref, g_ref, w1_ref, w3_ref, w2_ref, o_ref, xn_ref, acc_ref):
    f = pl.program_id(1)

    @pl.when(f == 0)
    def _():
        xn_ref[...] = _rmsnorm(x_ref[...], g_ref[...]).astype(BF16)
        acc_ref[...] = jnp.zeros_like(acc_ref)

    act = _swiglu_act(xn_ref[...], w1_ref[...], w3_ref[...])
    acc_ref[...] += jnp.dot(act, w2_ref[...], preferred_element_type=F32)

    @pl.when(f == pl.num_programs(1) - 1)
    def _():
        o_ref[...] = x_ref[...] + acc_ref[...]


def dense_ffn(x, g, w1, w3, w2):
    n, d = x.shape
    dff = w1.shape[1]
    tm, tf = _tile(n, 512), _tile(dff, 512)
    vmem = 4 * tm * d * 4 + tm * d * 2 + tm * d * 4 + 2 * 3 * d * tf * 2 + 3 * tm * tf * 4
    return pl.pallas_call(
        _dense_ffn_kernel,
        grid=(n // tm, dff // tf),
        in_specs=[pl.BlockSpec((tm, d), lambda i, f: (i, 0)),
                  pl.BlockSpec((1, d), lambda i, f: (0, 0)),
                  pl.BlockSpec((d, tf), lambda i, f: (0, f)),
                  pl.BlockSpec((d, tf), lambda i, f: (0, f)),
                  pl.BlockSpec((tf, d), lambda i, f: (f, 0))],
        out_specs=pl.BlockSpec((tm, d), lambda i, f: (i, 0)),
        out_shape=jax.ShapeDtypeStruct((n, d), F32),
        scratch_shapes=[pltpu.VMEM((tm, d), BF16), pltpu.VMEM((tm, d), F32)],
        compiler_params=_params(("arbitrary", "arbitrary"), vmem + (8 << 20)),
        name="dense_ffn",
    )(x, g.reshape(1, d), w1, w3, w2)


def _nt_dot(a, b):
    return lax.dot_general(a, b, (((1,), (1,)), ((), ())), preferred_element_type=F32)


def _moba_kernel(slope_ref, q_ref, k_ref, v_ref, o_ref,
                 kmh_ref, kml_ref, m_ref, l_ref, acc_ref, *, n_blocks):
    h = pl.program_id(1)
    i = pl.program_id(2)
    tq = MOBA_BLOCK
    scale = HEAD_DIM ** -0.5
    slope = slope_ref[h]

    @pl.when(i == 0)
    def _():
        kmh_ref[...] = jnp.zeros_like(kmh_ref)
        kml_ref[...] = jnp.zeros_like(kml_ref)

        def mean_block(n, carry):
            kb = k_ref[0, pl.ds(pl.multiple_of(n * MOBA_BLOCK, MOBA_BLOCK), MOBA_BLOCK), :].astype(F32)
            km = jnp.sum(kb, axis=0, keepdims=True) * (1.0 / MOBA_BLOCK)
            hi = km.astype(BF16)
            kmh_ref[pl.ds(n, 1), :] = hi.astype(F32)
            kml_ref[pl.ds(n, 1), :] = km - hi.astype(F32)
            return carry

        lax.fori_loop(0, n_blocks, mean_block, 0)

    q = q_ref[0]
    nbp = kmh_ref.shape[0]
    gate = _nt_dot(q, kmh_ref[...].astype(BF16)) + _nt_dot(q, kml_ref[...].astype(BF16))
    lane = lax.broadcasted_iota(jnp.int32, (tq, nbp), 1)
    gate = jnp.where(lane < i, gate, NEG_INF)
    sel = []
    for kk in range(min(MOBA_TOPK, n_blocks)):
        mx = jnp.max(gate, axis=-1, keepdims=True)
        idx = jnp.min(jnp.where(gate == mx, lane, nbp), axis=-1, keepdims=True)
        sel.append(jnp.where(kk < i, idx, -1))
        gate = jnp.where(lane == idx, -jnp.inf, gate)

    row = lax.broadcasted_iota(jnp.int32, (tq, MOBA_BLOCK), 0)
    col = lax.broadcasted_iota(jnp.int32, (tq, MOBA_BLOCK), 1)
    rel = (row - col).astype(F32)

    base = pl.multiple_of(i * MOBA_BLOCK, MOBA_BLOCK)
    s = _nt_dot(q, k_ref[0, pl.ds(base, MOBA_BLOCK), :]) * scale
    s = jnp.where(row >= col, s - slope * rel, NEG_INF)
    m0 = jnp.max(s, axis=-1, keepdims=True)
    p = jnp.exp(s - m0)
    m_ref[...] = m0
    l_ref[...] = jnp.sum(p, axis=-1, keepdims=True)
    acc_ref[...] = jnp.dot(p.astype(BF16), v_ref[0, pl.ds(base, MOBA_BLOCK), :], preferred_element_type=F32)

    def past_block(j, carry):
        kbase = pl.multiple_of(j * MOBA_BLOCK, MOBA_BLOCK)
        picked = sel[0] == j
        for sk in sel[1:]:
            picked = picked | (sk == j)
        dist0 = ((i - j) * MOBA_BLOCK).astype(F32)
        sj = _nt_dot(q, k_ref[0, pl.ds(kbase, MOBA_BLOCK), :]) * scale
        sj = jnp.where(picked, sj - slope * (rel + dist0), NEG_INF)
        m_old = m_ref[...]
        m_new = jnp.maximum(m_old, jnp.max(sj, axis=-1, keepdims=True))
        alpha = jnp.exp(m_old - m_new)
        pj = jnp.exp(sj - m_new)
        m_ref[...] = m_new
        l_ref[...] = alpha * l_ref[...] + jnp.sum(pj, axis=-1, keepdims=True)
        acc_ref[...] = alpha * acc_ref[...] + jnp.dot(
            pj.astype(BF16), v_ref[0, pl.ds(kbase, MOBA_BLOCK), :], preferred_element_type=F32)
        return carry

    lax.fori_loop(0, i, past_block, 0)
    o_ref[...] = (acc_ref[...] / l_ref[...]).astype(o_ref.dtype)


def moba_attention(qkv, batch, seq, n_heads):
    assert seq % MOBA_BLOCK == 0
    n = batch * seq
    nqb = seq // MOBA_BLOCK
    nbp = -(-nqb // LANES) * LANES
    slopes = jnp.asarray([2.0 ** (-8.0 * (h + 1) / n_heads) for h in range(n_heads)], dtype=F32)
    tq = MOBA_BLOCK
    vmem = 2 * (2 * seq * HEAD_DIM * 2) + 4 * tq * HEAD_DIM * 2 + 2 * nbp * HEAD_DIM * 4 \
        + 12 * tq * MOBA_BLOCK * 4
    grid_spec = pltpu.PrefetchScalarGridSpec(
        num_scalar_prefetch=1,
        grid=(batch, n_heads, nqb),
        in_specs=[pl.BlockSpec((1, tq, HEAD_DIM), lambda b, h, i, s: (h, b * nqb + i, 0)),
                  pl.BlockSpec((1, seq, HEAD_DIM), lambda b, h, i, s: (n_heads + h, b, 0)),
                  pl.BlockSpec((1, seq, HEAD_DIM), lambda b, h, i, s: (2 * n_heads + h, b, 0))],
        out_specs=pl.BlockSpec((tq, HEAD_DIM), lambda b, h, i, s: (b * nqb + i, h)),
        scratch_shapes=[pltpu.VMEM((nbp, HEAD_DIM), F32), pltpu.VMEM((nbp, HEAD_DIM), F32),
                        pltpu.VMEM((tq, 1), F32), pltpu.VMEM((tq, 1), F32),
                        pltpu.VMEM((tq, HEAD_DIM), F32)],
    )
    return pl.pallas_call(
        functools.partial(_moba_kernel, n_blocks=nqb),
        grid_spec=grid_spec,
        out_shape=jax.ShapeDtypeStruct((n, n_heads * HEAD_DIM), BF16),
        compiler_params=_params(("arbitrary", "arbitrary", "arbitrary"), vmem + (8 << 20)),
        name="moba",
    )(slopes, qkv, qkv, qkv)


SB_BLOCK = 256


def _stickbreak_kernel(q_ref, k_ref, v_ref, o_ref, carry_ref, acc_ref):
    i = pl.program_id(2)
    t = SB_BLOCK
    scale = HEAD_DIM ** -0.5
    q = q_ref[0]
    row = lax.broadcasted_iota(jnp.int32, (t, t), 0)
    col = lax.broadcasted_iota(jnp.int32, (t, t), 1)
    later = (row > col).astype(BF16)
    causal = col < row

    acc_ref[...] = jnp.zeros_like(acc_ref)
    carry_ref[...] = jnp.zeros_like(carry_ref)

    def key_block(jb, diagonal):
        base = pl.multiple_of(jb * t, t)
        z = _nt_dot(q, k_ref[0, pl.ds(base, t), :]) * scale
        log_beta = jnp.minimum(z, 0.0) - jnp.log1p(jnp.exp(-jnp.abs(z)))
        log_keep = log_beta - z
        if diagonal:
            log_keep = jnp.where(causal, log_keep, 0.0)
        hi = log_keep.astype(BF16)
        lo = (log_keep - hi.astype(F32)).astype(BF16)
        suffix = jnp.dot(hi, later, preferred_element_type=F32) + jnp.dot(lo, later, preferred_element_type=F32)
        w = jnp.exp(log_beta + suffix + carry_ref[...])
        if diagonal:
            w = jnp.where(causal, w, 0.0)
        acc_ref[...] += jnp.dot(w.astype(BF16), v_ref[0, pl.ds(base, t), :], preferred_element_type=F32)
        carry_ref[...] += jnp.sum(log_keep, axis=-1, keepdims=True)

    key_block(i, True)

    def cond(state):
        jb, worst = state
        return jnp.logical_and(jb >= 0, worst > F32_EXP_ZERO_BELOW)

    def body(state):
        jb, _ = state
        key_block(jb, False)
        return jb - 1, jnp.max(carry_ref[...])

    lax.while_loop(cond, body, (i - 1, jnp.max(carry_ref[...])))
    o_ref[...] = acc_ref[...].astype(o_ref.dtype)


def stickbreak_attention(q, kv, batch, seq, n_heads):
    assert seq % SB_BLOCK == 0
    n = batch * seq
    nqb = seq // SB_BLOCK
    t = SB_BLOCK
    vmem = 2 * (2 * seq * HEAD_DIM * 2) + 4 * t * HEAD_DIM * 2 + 14 * t * t * 4
    return pl.pallas_call(
        _stickbreak_kernel,
        grid=(batch, n_heads, nqb),
        in_specs=[pl.BlockSpec((1, t, HEAD_DIM), lambda b, h, i: (h, b * nqb + i, 0)),
                  pl.BlockSpec((1, seq, HEAD_DIM), lambda b, h, i: (h, b, 0)),
                  pl.BlockSpec((1, seq, HEAD_DIM), lambda b, h, i: (n_heads + h, b, 0))],
        out_specs=pl.BlockSpec((t, HEAD_DIM), lambda b, h, i: (b * nqb + i, h)),
        out_shape=jax.ShapeDtypeStruct((n, n_heads * HEAD_DIM), BF16),
        scratch_shapes=[pltpu.VMEM((t, 1), F32), pltpu.VMEM((t, HEAD_DIM), F32)],
        compiler_params=_params(("arbitrary", "arbitrary", "arbitrary"), vmem + (8 << 20)),
        name="stickbreak",
    )(q, kv, kv)


def _route_kernel(x_ref, g_ref, wh_ref, wl_ref, xn_ref, r_ref, *, n_experts):
    xn = _rmsnorm(x_ref[...], g_ref[...])
    xn_ref[...] = xn
    hi = xn.astype(BF16)
    lo = (xn - hi.astype(F32)).astype(BF16)
    wh, wl = wh_ref[...], wl_ref[...]
    logits = (jnp.dot(hi, wh, preferred_element_type=F32) + jnp.dot(hi, wl, preferred_element_type=F32)
              + jnp.dot(lo, wh, preferred_element_type=F32))
    lane = lax.broadcasted_iota(jnp.int32, logits.shape, 1)
    logits = jnp.where(lane < n_experts, logits, -jnp.inf)
    v1 = jnp.max(logits, axis=-1, keepdims=True)
    i1 = jnp.min(jnp.where(logits == v1, lane, LANES), axis=-1, keepdims=True)
    rest = jnp.where(lane == i1, -jnp.inf, logits)
    v2 = jnp.max(rest, axis=-1, keepdims=True)
    i2 = jnp.min(jnp.where(rest == v2, lane, LANES), axis=-1, keepdims=True)
    e2 = jnp.exp(v2 - v1)
    w1 = 1.0 / (1.0 + e2)
    w2 = e2 / (1.0 + e2)
    out = jnp.where(lane == 0, i1.astype(F32),
                    jnp.where(lane == 1, i2.astype(F32),
                              jnp.where(lane == 2, w1, jnp.where(lane == 3, w2, 0.0))))
    r_ref[...] = out


def route(x, g, w_router):
    n, d = x.shape
    n_experts = w_router.shape[1]
    assert n_experts <= LANES
    wpad = jnp.zeros((d, LANES), F32).at[:, :n_experts].set(w_router)
    wh = wpad.astype(BF16)
    wl = (wpad - wh.astype(F32)).astype(BF16)
    tm = _tile(n, 512)
    vmem = 4 * tm * d * 4 + 4 * tm * d * 4 + 4 * d * LANES * 2 + 4 * tm * LANES * 4
    return pl.pallas_call(
        functools.partial(_route_kernel, n_experts=n_experts),
        grid=(n // tm,),
        in_specs=[pl.BlockSpec((tm, d), lambda i: (i, 0)),
                  pl.BlockSpec((1, d), lambda i: (0, 0)),
                  pl.BlockSpec((d, LANES), lambda i: (0, 0)),
                  pl.BlockSpec((d, LANES), lambda i: (0, 0))],
        out_specs=[pl.BlockSpec((tm, d), lambda i: (i, 0)),
                   pl.BlockSpec((tm, LANES), lambda i: (i, 0))],
        out_shape=[jax.ShapeDtypeStruct((n, d), F32), jax.ShapeDtypeStruct((n, LANES), F32)],
        compiler_params=_params(("arbitrary",), vmem + (8 << 20)),
        name="route",
    )(x, g.reshape(1, d), wh, wl)


def _row_copy(src_hbm, idx, dst_ref, r, sem):
    return pltpu.make_async_copy(src_hbm.at[pl.ds(idx, 1), :], dst_ref.at[pl.ds(r, 1), :], sem)


def _gather_rows_kernel(idx_ref, src_hbm, o_ref, buf_ref, sem):
    rows = buf_ref.shape[0]

    def start(r, carry):
        _row_copy(src_hbm, idx_ref[0, 0, r], buf_ref, r, sem).start()
        return carry

    def wait(r, carry):
        _row_copy(src_hbm, 0, buf_ref, r, sem).wait()
        return carry

    lax.fori_loop(0, rows, start, 0, unroll=8)
    lax.fori_loop(0, rows, wait, 0, unroll=8)
    o_ref[...] = buf_ref[...].astype(o_ref.dtype)


def gather_rows(src, idx, rows_per_step, out_dtype):
    n_out = idx.shape[0]
    d = src.shape[1]
    r = rows_per_step
    assert n_out % r == 0
    vmem = r * d * 4 + 2 * r * d * 4
    return pl.pallas_call(
        _gather_rows_kernel,
        grid=(n_out // r,),
        in_specs=[pl.BlockSpec((1, 1, r), lambda i: (i, 0, 0), memory_space=pltpu.SMEM),
                  pl.BlockSpec(memory_space=pl.ANY)],
        out_specs=pl.BlockSpec((r, d), lambda i: (i, 0)),
        out_shape=jax.ShapeDtypeStruct((n_out, d), out_dtype),
        scratch_shapes=[pltpu.VMEM((r, d), src.dtype), pltpu.SemaphoreType.DMA(())],
        compiler_params=_params(("arbitrary",), vmem + (8 << 20)),
        name="gather_rows",
    )(idx.reshape(n_out // r, 1, r), src)


def _expert_ffn_kernel(texp_ref, tvalid_ref, xs_ref, w1_ref, w3_ref, w2_ref, o_ref):
    t = pl.program_id(0)
    f = pl.program_id(1)

    @pl.when(f == 0)
    def _():
        o_ref[...] = jnp.zeros_like(o_ref)

    @pl.when(tvalid_ref[t] == 1)
    def _():
        act = _swiglu_act(xs_ref[...], w1_ref[0], w3_ref[0])
        o_ref[...] += jnp.dot(act, w2_ref[0], preferred_element_type=F32)


def expert_ffn(xs, tile_expert, tile_valid, w1, w3, w2, tm):
    p, d = xs.shape
    dff = w1.shape[2]
    tf = _tile(dff, 512)
    nf = dff // tf
    n_tiles = p // tm

    def wcol(t, f, te, tv):
        return (te[t], 0, jnp.where(tv[t] == 1, f, nf - 1))

    def wrow(t, f, te, tv):
        return (te[t], jnp.where(tv[t] == 1, f, nf - 1), 0)

    vmem = 2 * tm * d * 2 + 2 * tm * d * 4 + 2 * 3 * d * tf * 2 + 3 * tm * tf * 4
    grid_spec = pltpu.PrefetchScalarGridSpec(
        num_scalar_prefetch=2,
        grid=(n_tiles, nf),
        in_specs=[pl.BlockSpec((tm, d), lambda t, f, te, tv: (t, 0)),
                  pl.BlockSpec((1, d, tf), wcol),
                  pl.BlockSpec((1, d, tf), wcol),
                  pl.BlockSpec((1, tf, d), wrow)],
        out_specs=pl.BlockSpec((tm, d), lambda t, f, te, tv: (t, 0)),
    )
    return pl.pallas_call(
        _expert_ffn_kernel,
        grid_spec=grid_spec,
        out_shape=jax.ShapeDtypeStruct((p, d), F32),
        compiler_params=_params(("arbitrary", "arbitrary"), vmem + (8 << 20)),
        name="expert_ffn",
    )(tile_expert, tile_valid, xs, w1, w3, w2)


def _combine_kernel(pos_ref, x_ref, r_ref, g_ref, ys_hbm, o_ref, buf_ref, sem):
    tc = x_ref.shape[0]

    def start(r, carry):
        _row_copy(ys_hbm, pos_ref[0, 0, r], buf_ref, r, sem).start()
        return carry

    def wait(r, carry):
        _row_copy(ys_hbm, 0, buf_ref, r, sem).wait()
        return carry

    lax.fori_loop(0, MOE_TOPK * tc, start, 0, unroll=8)
    lax.fori_loop(0, MOE_TOPK * tc, wait, 0, unroll=8)
    h = x_ref[...]
    for k in range(MOE_TOPK):
        h = h + r_ref[:, MOE_TOPK + k:MOE_TOPK + k + 1] * buf_ref[pl.ds(k * tc, tc), :]
    o_ref[...] = _rmsnorm(h, g_ref[...])


def combine(x, route_info, pos, ys, g):
    n, d = x.shape
    tc = pos.shape[2] // MOE_TOPK
    vmem = 4 * tc * d * 4 + MOE_TOPK * tc * d * 4 + 2 * tc * LANES * 4 + 3 * tc * d * 4
    return pl.pallas_call(
        _combine_kernel,
        grid=(n // tc,),
        in_specs=[pl.BlockSpec((1, 1, MOE_TOPK * tc), lambda i: (i, 0, 0), memory_space=pltpu.SMEM),
                  pl.BlockSpec((tc, d), lambda i: (i, 0)),
                  pl.BlockSpec((tc, LANES), lambda i: (i, 0)),
                  pl.BlockSpec((1, d), lambda i: (0, 0)),
                  pl.BlockSpec(memory_space=pl.ANY)],
        out_specs=pl.BlockSpec((tc, d), lambda i: (i, 0)),
        out_shape=jax.ShapeDtypeStruct((n, d), F32),
        scratch_shapes=[pltpu.VMEM((MOE_TOPK * tc, d), F32), pltpu.SemaphoreType.DMA(())],
        compiler_params=_params(("arbitrary",), vmem + (8 << 20)),
        name="combine",
    )(pos, x, route_info, g.reshape(1, d), ys)


def _routing_plan(experts, n_experts, tm):
    n = experts.shape[0]
    flat = experts.T.reshape(-1)
    onehot = (flat[:, None] == jnp.arange(n_experts, dtype=jnp.int32)[None, :]).astype(jnp.int32)
    csum = jnp.cumsum(onehot, axis=0)
    counts = csum[-1]
    rank = jnp.take_along_axis(csum, flat[:, None], axis=1)[:, 0] - 1
    padded = (counts + tm - 1) // tm * tm
    gend = jnp.cumsum(padded)
    pos = (gend - padded)[flat] + rank
    p_rows = MOE_TOPK * n + n_experts * tm
    tokens = jnp.tile(jnp.arange(n, dtype=jnp.int32), MOE_TOPK)
    src = jnp.zeros((p_rows,), jnp.int32).at[pos].set(tokens)
    tile_start = jnp.arange(p_rows // tm, dtype=jnp.int32) * tm
    tile_valid = (tile_start < gend[-1]).astype(jnp.int32)
    tile_expert = jnp.searchsorted(gend, tile_start, side="right").astype(jnp.int32)
    last_expert = jnp.max(jnp.where(counts > 0, jnp.arange(n_experts, dtype=jnp.int32), 0))
    tile_expert = jnp.where(tile_valid == 1, tile_expert, last_expert)
    return src, pos.astype(jnp.int32), tile_expert, tile_valid


def moe_block(h, g_ffn, w_router, w1, w3, w2, g_final):
    n, d = h.shape
    n_experts = w_router.shape[1]
    tm = _tile(MOE_TOPK * n, 512)
    xn, route_info = route(h, g_ffn, w_router)
    experts = route_info[:, :MOE_TOPK].astype(jnp.int32)
    src, pos, tile_expert, tile_valid = _routing_plan(experts, n_experts, tm)
    xs = gather_rows(xn, src, _tile(src.shape[0], 512), BF16)
    ys = expert_ffn(xs, tile_expert, tile_valid, w1, w3, w2, tm)
    tc = _tile(n, 256)
    pos_tiles = jnp.stack([pos[k * n:(k + 1) * n].reshape(n // tc, tc) for k in range(MOE_TOPK)], axis=1)
    return combine(h, route_info, pos_tiles.reshape(n // tc, 1, MOE_TOPK * tc), ys, g_final)


def kernel(x, attn_norm_g, ffn_norm_g, a_w_qkv, a_w_o, kv_norm_g, b_w_kv, b_w_q, b_w_o,
           dense_w1, dense_w3, dense_w2, moe_w_router, moe_w1, moe_w3, moe_w2, final_norm_g):
    assert a_w_qkv.shape[0] == 1 and b_w_q.shape[0] == 1 and dense_w1.shape[0] == 1 and moe_w1.shape[0] == 1
    batch, seq, d = x.shape
    n = batch * seq
    heads_a = a_w_qkv.shape[2] // (3 * HEAD_DIM)
    heads_b = b_w_q.shape[2] // HEAD_DIM
    bf = lambda w: w.astype(BF16)
    h = x.reshape(n, d)

    qkv = norm_matmul_heads(h, attn_norm_g[0], bf(a_w_qkv[0]))
    mix = moba_attention(qkv, batch, seq, heads_a)
    h = matmul_res(mix, bf(a_w_o[0]), h)
    h = dense_ffn(h, ffn_norm_g[0], bf(dense_w1[0]), bf(dense_w3[0]), bf(dense_w2[0]))

    kv = norm_matmul_heads(h, kv_norm_g, bf(b_w_kv))
    q = norm_matmul_heads(h, attn_norm_g[1], bf(b_w_q[0]))
    mix = stickbreak_attention(q, kv, batch, seq, heads_b)
    h = matmul_res(mix, bf(b_w_o[0]), h)
    out = moe_block(h, ffn_norm_g[1], moe_w_router[0], bf(moe_w1[0]), bf(moe_w3[0]), bf(moe_w2[0]),
                    final_norm_g)
    return out.reshape(batch, seq, d)
```

```python
import functools

import jax
import jax.numpy as jnp
from jax import lax
from jax.experimental import pallas as pl
from jax.experimental.pallas import tpu as pltpu

HEAD_DIM = 128
MOBA_BLOCK = 256
MOBA_TOPK = 3
MOE_TOPK = 2
RMS_EPS = 1e-6
NEG_INF = -1e30
LANES = 128
VMEM_CAP = 60 * 1024 * 1024
F32_EXP_ZERO_BELOW = -104.0

F32 = jnp.float32
BF16 = jnp.bfloat16


def _tile(dim, pref):
    t = min(dim, pref)
    while dim % t:
        t //= 2
    return t


def _params(sem, vmem_bytes):
    return pltpu.CompilerParams(dimension_semantics=sem,
                                vmem_limit_bytes=int(min(VMEM_CAP, max(vmem_bytes, 16 * 1024 * 1024))))


def _rmsnorm(x, g):
    return x * lax.rsqrt(jnp.mean(x * x, axis=-1, keepdims=True) + RMS_EPS) * g


def _norm_matmul_kernel(x_ref, g_ref, w_ref, o_ref, xn_ref):
    @pl.when(pl.program_id(1) == 0)
    def _():
        xn_ref[...] = _rmsnorm(x_ref[...], g_ref[...]).astype(BF16)

    acc = jnp.dot(xn_ref[...], w_ref[...], preferred_element_type=F32)
    for c in range(o_ref.shape[0]):
        o_ref[c] = acc[:, c * HEAD_DIM:(c + 1) * HEAD_DIM].astype(o_ref.dtype)


def norm_matmul_heads(x, g, w):
    n, d = x.shape
    dout = w.shape[1]
    tm, tn = _tile(n, 1024), _tile(dout, 512)
    vmem = 2 * tm * d * 4 + tm * d * 2 + 2 * d * tn * 2 + 2 * tm * tn * 2 + 2 * tm * tn * 4
    return pl.pallas_call(
        _norm_matmul_kernel,
        grid=(n // tm, dout // tn),
        in_specs=[pl.BlockSpec((tm, d), lambda i, j: (i, 0)),
                  pl.BlockSpec((1, d), lambda i, j: (0, 0)),
                  pl.BlockSpec((d, tn), lambda i, j: (0, j))],
        out_specs=pl.BlockSpec((tn // HEAD_DIM, tm, HEAD_DIM), lambda i, j: (j, i, 0)),
        out_shape=jax.ShapeDtypeStruct((dout // HEAD_DIM, n, HEAD_DIM), BF16),
        scratch_shapes=[pltpu.VMEM((tm, d), BF16)],
        compiler_params=_params(("arbitrary", "arbitrary"), vmem + (8 << 20)),
        name="norm_matmul",
    )(x, g.reshape(1, d), w)


def _norm_matmul_t_kernel(x_ref, g_ref, wt_ref, o_ref, xn_ref):
    @pl.when(pl.program_id(1) == 0)
    def _():
        xn_ref[...] = _rmsnorm(x_ref[...], g_ref[...]).astype(BF16)

    o_ref[...] = _nt_dot(wt_ref[...], xn_ref[...]).astype(o_ref.dtype)


def norm_matmul_t(x, g, wt):
    n, d = x.shape
    dout = wt.shape[0]
    tm, tn = _tile(n, 1024), _tile(dout, 512)
    vmem = 2 * tm * d * 4 + tm * d * 2 + 2 * d * tn * 2 + 2 * tm * tn * 2 + 2 * tm * tn * 4
    return pl.pallas_call(
        _norm_matmul_t_kernel,
        grid=(n // tm, dout // tn),
        in_specs=[pl.BlockSpec((tm, d), lambda i, j: (i, 0)),
                  pl.BlockSpec((1, d), lambda i, j: (0, 0)),
                  pl.BlockSpec((tn, d), lambda i, j: (j, 0))],
        out_specs=pl.BlockSpec((tn, tm), lambda i, j: (j, i)),
        out_shape=jax.ShapeDtypeStruct((dout, n), BF16),
        scratch_shapes=[pltpu.VMEM((tm, d), BF16)],
        compiler_params=_params(("arbitrary", "arbitrary"), vmem + (8 << 20)),
        name="norm_matmul_t",
    )(x, g.reshape(1, d), wt)


def _matmul_res_kernel(a_ref, w_ref, r_ref, o_ref):
    o_ref[...] = r_ref[...] + jnp.dot(a_ref[...], w_ref[...], preferred_element_type=F32)


def matmul_res(a, w, res):
    n, k = a.shape
    dout = w.shape[1]
    tm, tn = _tile(n, 1024), _tile(dout, 512)
    vmem = 2 * (tm * k * 2 + k * tn * 2 + 2 * tm * tn * 4) + tm * tn * 4
    return pl.pallas_call(
        _matmul_res_kernel,
        grid=(n // tm, dout // tn),
        in_specs=[pl.BlockSpec((tm, k), lambda i, j: (i, 0)),
                  pl.BlockSpec((k, tn), lambda i, j: (0, j)),
                  pl.BlockSpec((tm, tn), lambda i, j: (i, j))],
        out_specs=pl.BlockSpec((tm, tn), lambda i, j: (i, j)),
        out_shape=jax.ShapeDtypeStruct((n, dout), F32),
        compiler_params=_params(("arbitrary", "arbitrary"), vmem + (8 << 20)),
        name="matmul_res",
    )(a, w, res)


def _swiglu_act(xn, w1, w3):
    gate = jnp.dot(xn, w1, preferred_element_type=F32)
    up = jnp.dot(xn, w3, preferred_element_type=F32)
    return (gate / (1.0 + jnp.exp(-gate)) * up).astype(BF16)


def _dense_ffn_kernel(x_ref, g_ref, w1_ref, w3_ref, w2_ref, o_ref, xn_ref, acc_ref):
    f = pl.program_id(1)

    @pl.when(f == 0)
    def _():
        xn_ref[...] = _rmsnorm(x_ref[...], g_ref[...]).astype(BF16)
        acc_ref[...] = jnp.zeros_like(acc_ref)

    act = _swiglu_act(xn_ref[...], w1_ref[...], w3_ref[...])
    acc_ref[...] += jnp.dot(act, w2_ref[...], preferred_element_type=F32)

    @pl.when(f == pl.num_programs(1) - 1)
    def _():
        o_ref[...] = x_ref[...] + acc_ref[...]


def dense_ffn(x, g, w1, w3, w2):
    n, d = x.shape
    dff = w1.shape[1]
    tm, tf = _tile(n, 512), _tile(dff, 512)
    vmem = 4 * tm * d * 4 + tm * d * 2 + tm * d * 4 + 2 * 3 * d * tf * 2 + 3 * tm * tf * 4
    return pl.pallas_call(
        _dense_ffn_kernel,
        grid=(n // tm, dff // tf),
        in_specs=[pl.BlockSpec((tm, d), lambda i, f: (i, 0)),
                  pl.BlockSpec((1, d), lambda i, f: (0, 0)),
                  pl.BlockSpec((d, tf), lambda i, f: (0, f)),
                  pl.BlockSpec((d, tf), lambda i, f: (0, f)),
                  pl.BlockSpec((tf, d), lambda i, f: (f, 0))],
        out_specs=pl.BlockSpec((tm, d), lambda i, f: (i, 0)),
        out_shape=jax.ShapeDtypeStruct((n, d), F32),
        scratch_shapes=[pltpu.VMEM((tm, d), BF16), pltpu.VMEM((tm, d), F32)],
        compiler_params=_params(("arbitrary", "arbitrary"), vmem + (8 << 20)),
        name="dense_ffn",
    )(x, g.reshape(1, d), w1, w3, w2)


def _nt_dot(a, b):
    return lax.dot_general(a, b, (((1,), (1,)), ((), ())), preferred_element_type=F32)


LOG2E = 1.4426950408889634
PAST_TILE = 4


def _moba_kernel(slope_ref, q_ref, k_ref, vt_ref, o_ref,
                 kmh_ref, kml_ref, pick_ref, bias_ref, m_ref, l_ref, acc_ref, *, n_blocks):
    h = pl.program_id(1)
    i = pl.program_id(2)
    blk = MOBA_BLOCK
    slope2 = slope_ref[h] * LOG2E
    scale2 = HEAD_DIM ** -0.5 * LOG2E

    @pl.when(i == 0)
    def _():
        kmh_ref[...] = jnp.zeros_like(kmh_ref)
        kml_ref[...] = jnp.zeros_like(kml_ref)

        def mean_block(n, carry):
            kb = k_ref[0, pl.ds(pl.multiple_of(n * blk, blk), blk), :].astype(F32)
            km = jnp.sum(kb, axis=0, keepdims=True) * (1.0 / blk)
            hi = km.astype(BF16).astype(F32)
            kmh_ref[pl.ds(n, 1), :] = hi
            kml_ref[pl.ds(n, 1), :] = km - hi
            return carry

        lax.fori_loop(0, n_blocks, mean_block, 0)
        key = lax.broadcasted_iota(jnp.int32, bias_ref.shape, 0)
        qry = lax.broadcasted_iota(jnp.int32, bias_ref.shape, 1)
        bias_ref[...] = slope2 * (qry - key).astype(F32)

    q = q_ref[0]
    nbp = kmh_ref.shape[0]
    gate = _nt_dot(kmh_ref[...].astype(BF16), q) + _nt_dot(kml_ref[...].astype(BF16), q)
    sub = lax.broadcasted_iota(jnp.int32, (nbp, blk), 0)
    gate = jnp.where(sub < i, gate, NEG_INF)
    pick = jnp.zeros((nbp, blk), F32)
    for kk in range(min(MOBA_TOPK, n_blocks)):
        mx = jnp.max(gate, axis=0, keepdims=True)
        idx = jnp.min(jnp.where(gate == mx, sub, nbp), axis=0, keepdims=True)
        pick = jnp.where(sub == jnp.where(kk < i, idx, -1), 1.0, pick)
        gate = jnp.where(sub == idx, -jnp.inf, gate)
    pick_ref[...] = pick

    base = pl.multiple_of(i * blk, blk)
    key = lax.broadcasted_iota(jnp.int32, (blk, blk), 0)
    qry = lax.broadcasted_iota(jnp.int32, (blk, blk), 1)
    s = _nt_dot(k_ref[0, pl.ds(base, blk), :], q) * scale2 - bias_ref[pl.ds(0, blk), :]
    s = jnp.where(qry >= key, s, NEG_INF)
    m0 = jnp.max(s, axis=0, keepdims=True)
    p = jnp.exp2(s - m0)
    m_ref[...] = m0
    l_ref[...] = jnp.sum(p, axis=0, keepdims=True)
    acc_ref[...] = jnp.dot(vt_ref[:, pl.ds(base, blk)], p.astype(BF16), preferred_element_type=F32)

    def past_tile(t, carry):
        j0 = t * PAST_TILE
        kbase = pl.multiple_of(j0 * blk, blk)
        width = PAST_TILE * blk
        sj = _nt_dot(k_ref[0, pl.ds(kbase, width), :], q) * scale2 - bias_ref[...]
        parts = []
        for u in range(PAST_TILE):
            picked = pick_ref[pl.ds(j0 + u, 1), :] > 0.5
            parts.append(jnp.where(picked, sj[u * blk:(u + 1) * blk], NEG_INF))
        sj = jnp.concatenate(parts, axis=0)
        shift = slope2 * ((i - j0) * blk).astype(F32)
        m_old = m_ref[...]
        m_new = jnp.maximum(m_old, jnp.max(sj, axis=0, keepdims=True) - shift)
        pj = jnp.exp2(sj - (m_new + shift))
        alpha = jnp.exp2(m_old - m_new)
        m_ref[...] = m_new
        l_ref[...] = alpha * l_ref[...] + jnp.sum(pj, axis=0, keepdims=True)
        acc_ref[...] = alpha * acc_ref[...] + jnp.dot(
            vt_ref[:, pl.ds(kbase, width)], pj.astype(BF16), preferred_element_type=F32)
        return carry

    lax.fori_loop(0, (i + PAST_TILE - 1) // PAST_TILE, past_tile, 0)
    o_ref[...] = (acc_ref[...] / l_ref[...]).T.astype(o_ref.dtype)


def moba_attention(qk, vt, batch, seq, n_heads):
    assert seq % (PAST_TILE * MOBA_BLOCK) == 0
    n = batch * seq
    nqb = seq // MOBA_BLOCK
    nbp = -(-nqb // 8) * 8
    slopes = jnp.asarray([2.0 ** (-8.0 * (h + 1) / n_heads) for h in range(n_heads)], dtype=F32)
    blk = MOBA_BLOCK
    vmem = 2 * (2 * seq * HEAD_DIM * 2) + 4 * blk * HEAD_DIM * 2 + 11 * PAST_TILE * blk * blk * 4
    grid_spec = pltpu.PrefetchScalarGridSpec(
        num_scalar_prefetch=1,
        grid=(batch, n_heads, nqb),
        in_specs=[pl.BlockSpec((1, blk, HEAD_DIM), lambda b, h, i, s: (h, b * nqb + i, 0)),
                  pl.BlockSpec((1, seq, HEAD_DIM), lambda b, h, i, s: (n_heads + h, b, 0)),
                  pl.BlockSpec((HEAD_DIM, seq), lambda b, h, i, s: (h, b))],
        out_specs=pl.BlockSpec((blk, HEAD_DIM), lambda b, h, i, s: (b * nqb + i, h)),
        scratch_shapes=[pltpu.VMEM((nbp, HEAD_DIM), F32), pltpu.VMEM((nbp, HEAD_DIM), F32),
                        pltpu.VMEM((nbp, blk), F32), pltpu.VMEM((PAST_TILE * blk, blk), F32),
                        pltpu.VMEM((1, blk), F32), pltpu.VMEM((1, blk), F32),
                        pltpu.VMEM((HEAD_DIM, blk), F32)],
    )
    return pl.pallas_call(
        functools.partial(_moba_kernel, n_blocks=nqb),
        grid_spec=grid_spec,
        out_shape=jax.ShapeDtypeStruct((n, n_heads * HEAD_DIM), BF16),
        compiler_params=_params(("arbitrary", "arbitrary", "arbitrary"), vmem + (8 << 20)),
        name="moba",
    )(slopes, qk, qk, vt)


SB_BLOCK = 256


def _stickbreak_kernel(q_ref, k_ref, v_ref, o_ref, carry_ref, acc_ref):
    i = pl.program_id(2)
    t = SB_BLOCK
    scale = HEAD_DIM ** -0.5
    q = q_ref[0]
    row = lax.broadcasted_iota(jnp.int32, (t, t), 0)
    col = lax.broadcasted_iota(jnp.int32, (t, t), 1)
    later = (row > col).astype(BF16)

    def log_sigmoid(z):
        return jnp.minimum(z, 0.0) - jnp.log1p(jnp.exp(-jnp.abs(z)))

    def suffix_in_block(log_keep):
        hi = log_keep.astype(BF16)
        lo = (log_keep - hi.astype(F32)).astype(BF16)
        return jnp.dot(hi, later, preferred_element_type=F32) + jnp.dot(lo, later, preferred_element_type=F32)

    first = jnp.maximum(i - 1, 0)
    kbase = pl.multiple_of(first * t, t)
    z = _nt_dot(q, k_ref[0, pl.ds(kbase, 2 * t), :]) * scale
    log_beta = log_sigmoid(z)
    qrow = lax.broadcasted_iota(jnp.int32, (t, 2 * t), 0)
    kcol = lax.broadcasted_iota(jnp.int32, (t, 2 * t), 1)
    causal = kcol + (first - i) * t < qrow
    log_keep = jnp.where(causal, log_beta - z, 0.0)
    tail_sum = jnp.sum(log_keep[:, t:], axis=-1, keepdims=True)
    suffix = suffix_in_block(jnp.concatenate([log_keep[:, :t], log_keep[:, t:]], axis=0))
    suffix = jnp.concatenate([suffix[:t] + tail_sum, suffix[t:]], axis=1)
    w = jnp.where(causal, jnp.exp(log_beta + suffix), 0.0)
    acc_ref[...] = jnp.dot(w.astype(BF16), v_ref[0, pl.ds(kbase, 2 * t), :], preferred_element_type=F32)
    carry_ref[...] = jnp.sum(log_keep[:, :t], axis=-1, keepdims=True) + tail_sum

    def key_block(jb):
        base = pl.multiple_of(jb * t, t)
        zb = _nt_dot(q, k_ref[0, pl.ds(base, t), :]) * scale
        lb = log_sigmoid(zb)
        lk = lb - zb
        wb = jnp.exp(lb + suffix_in_block(lk) + carry_ref[...])
        acc_ref[...] += jnp.dot(wb.astype(BF16), v_ref[0, pl.ds(base, t), :], preferred_element_type=F32)
        carry_ref[...] += jnp.sum(lk, axis=-1, keepdims=True)

    def cond(state):
        jb, worst = state
        return jnp.logical_and(jb >= 0, worst > F32_EXP_ZERO_BELOW)

    def body(state):
        jb, _ = state
        key_block(jb)
        return jb - 1, jnp.max(carry_ref[...])

    lax.while_loop(cond, body, (first - 1, jnp.max(carry_ref[...])))
    o_ref[...] = acc_ref[...].astype(o_ref.dtype)


def stickbreak_attention(q, kv, batch, seq, n_heads):
    assert seq % SB_BLOCK == 0 and seq >= 2 * SB_BLOCK
    n = batch * seq
    nqb = seq // SB_BLOCK
    t = SB_BLOCK
    vmem = 2 * (2 * seq * HEAD_DIM * 2) + 4 * t * HEAD_DIM * 2 + 14 * t * t * 4
    return pl.pallas_call(
        _stickbreak_kernel,
        grid=(batch, n_heads, nqb),
        in_specs=[pl.BlockSpec((1, t, HEAD_DIM), lambda b, h, i: (h, b * nqb + i, 0)),
                  pl.BlockSpec((1, seq, HEAD_DIM), lambda b, h, i: (h, b, 0)),
                  pl.BlockSpec((1, seq, HEAD_DIM), lambda b, h, i: (n_heads + h, b, 0))],
        out_specs=pl.BlockSpec((t, HEAD_DIM), lambda b, h, i: (b * nqb + i, h)),
        out_shape=jax.ShapeDtypeStruct((n, n_heads * HEAD_DIM), BF16),
        scratch_shapes=[pltpu.VMEM((t, 1), F32), pltpu.VMEM((t, HEAD_DIM), F32)],
        compiler_params=_params(("arbitrary", "arbitrary", "arbitrary"), vmem + (8 << 20)),
        name="stickbreak",
    )(q, kv, kv)


def _route_kernel(x_ref, g_ref, wh_ref, wl_ref, xn_ref, r_ref, *, n_experts):
    xn = _rmsnorm(x_ref[...], g_ref[...])
    xn_ref[...] = xn
    hi = xn.astype(BF16)
    lo = (xn - hi.astype(F32)).astype(BF16)
    wh, wl = wh_ref[...], wl_ref[...]
    logits = (jnp.dot(hi, wh, preferred_element_type=F32) + jnp.dot(hi, wl, preferred_element_type=F32)
              + jnp.dot(lo, wh, preferred_element_type=F32))
    lane = lax.broadcasted_iota(jnp.int32, logits.shape, 1)
    logits = jnp.where(lane < n_experts, logits, -jnp.inf)
    v1 = jnp.max(logits, axis=-1, keepdims=True)
    i1 = jnp.min(jnp.where(logits == v1, lane, LANES), axis=-1, keepdims=True)
    rest = jnp.where(lane == i1, -jnp.inf, logits)
    v2 = jnp.max(rest, axis=-1, keepdims=True)
    i2 = jnp.min(jnp.where(rest == v2, lane, LANES), axis=-1, keepdims=True)
    e2 = jnp.exp(v2 - v1)
    w1 = 1.0 / (1.0 + e2)
    w2 = e2 / (1.0 + e2)
    out = jnp.where(lane == 0, i1.astype(F32),
                    jnp.where(lane == 1, i2.astype(F32),
                              jnp.where(lane == 2, w1, jnp.where(lane == 3, w2, 0.0))))
    r_ref[...] = out


def route(x, g, w_router):
    n, d = x.shape
    n_experts = w_router.shape[1]
    assert n_experts <= LANES
    wpad = jnp.zeros((d, LANES), F32).at[:, :n_experts].set(w_router)
    wh = wpad.astype(BF16)
    wl = (wpad - wh.astype(F32)).astype(BF16)
    tm = _tile(n, 512)
    vmem = 4 * tm * d * 4 + 4 * tm * d * 4 + 4 * d * LANES * 2 + 4 * tm * LANES * 4
    return pl.pallas_call(
        functools.partial(_route_kernel, n_experts=n_experts),
        grid=(n // tm,),
        in_specs=[pl.BlockSpec((tm, d), lambda i: (i, 0)),
                  pl.BlockSpec((1, d), lambda i: (0, 0)),
                  pl.BlockSpec((d, LANES), lambda i: (0, 0)),
                  pl.BlockSpec((d, LANES), lambda i: (0, 0))],
        out_specs=[pl.BlockSpec((tm, d), lambda i: (i, 0)),
                   pl.BlockSpec((tm, LANES), lambda i: (i, 0))],
        out_shape=[jax.ShapeDtypeStruct((n, d), F32), jax.ShapeDtypeStruct((n, LANES), F32)],
        compiler_params=_params(("arbitrary",), vmem + (8 << 20)),
        name="route",
    )(x, g.reshape(1, d), wh, wl)


def _row_copy(src_hbm, idx, dst_ref, r, sem):
    return pltpu.make_async_copy(src_hbm.at[pl.ds(idx, 1), :], dst_ref.at[pl.ds(r, 1), :], sem)


def _gather_rows_kernel(idx_ref, src_hbm, o_ref, buf_ref, sem):
    rows = buf_ref.shape[0]

    def start(r, carry):
        _row_copy(src_hbm, idx_ref[0, 0, r], buf_ref, r, sem).start()
        return carry

    def wait(r, carry):
        _row_copy(src_hbm, 0, buf_ref, r, sem).wait()
        return carry

    lax.fori_loop(0, rows, start, 0, unroll=8)
    lax.fori_loop(0, rows, wait, 0, unroll=8)
    o_ref[...] = buf_ref[...].astype(o_ref.dtype)


def gather_rows(src, idx, rows_per_step, out_dtype):
    n_out = idx.shape[0]
    d = src.shape[1]
    r = rows_per_step
    assert n_out % r == 0
    vmem = r * d * 4 + 2 * r * d * 4
    return pl.pallas_call(
        _gather_rows_kernel,
        grid=(n_out // r,),
        in_specs=[pl.BlockSpec((1, 1, r), lambda i: (i, 0, 0), memory_space=pltpu.SMEM),
                  pl.BlockSpec(memory_space=pl.ANY)],
        out_specs=pl.BlockSpec((r, d), lambda i: (i, 0)),
        out_shape=jax.ShapeDtypeStruct((n_out, d), out_dtype),
        scratch_shapes=[pltpu.VMEM((r, d), src.dtype), pltpu.SemaphoreType.DMA(())],
        compiler_params=_params(("arbitrary",), vmem + (8 << 20)),
        name="gather_rows",
    )(idx.reshape(n_out // r, 1, r), src)


def _expert_ffn_kernel(texp_ref, tvalid_ref, xs_ref, w1_ref, w3_ref, w2_ref, o_ref):
    t = pl.program_id(0)
    f = pl.program_id(1)

    @pl.when(f == 0)
    def _():
        o_ref[...] = jnp.zeros_like(o_ref)

    @pl.when(tvalid_ref[t] == 1)
    def _():
        act = _swiglu_act(xs_ref[...], w1_ref[0], w3_ref[0])
        o_ref[...] += jnp.dot(act, w2_ref[0], preferred_element_type=F32)


def expert_ffn(xs, tile_expert, tile_valid, w1, w3, w2, tm):
    p, d = xs.shape
    dff = w1.shape[2]
    tf = _tile(dff, 512)
    nf = dff // tf
    n_tiles = p // tm

    def wcol(t, f, te, tv):
        return (te[t], 0, jnp.where(tv[t] == 1, f, nf - 1))

    def wrow(t, f, te, tv):
        return (te[t], jnp.where(tv[t] == 1, f, nf - 1), 0)

    vmem = 2 * tm * d * 2 + 2 * tm * d * 4 + 2 * 3 * d * tf * 2 + 3 * tm * tf * 4
    grid_spec = pltpu.PrefetchScalarGridSpec(
        num_scalar_prefetch=2,
        grid=(n_tiles, nf),
        in_specs=[pl.BlockSpec((tm, d), lambda t, f, te, tv: (t, 0)),
                  pl.BlockSpec((1, d, tf), wcol),
                  pl.BlockSpec((1, d, tf), wcol),
                  pl.BlockSpec((1, tf, d), wrow)],
        out_specs=pl.BlockSpec((tm, d), lambda t, f, te, tv: (t, 0)),
    )
    return pl.pallas_call(
        _expert_ffn_kernel,
        grid_spec=grid_spec,
        out_shape=jax.ShapeDtypeStruct((p, d), F32),
        compiler_params=_params(("arbitrary", "arbitrary"), vmem + (8 << 20)),
        name="expert_ffn",
    )(tile_expert, tile_valid, xs, w1, w3, w2)


def _combine_kernel(pos_ref, x_ref, r_ref, g_ref, ys_hbm, o_ref, buf_ref, sem):
    tc = x_ref.shape[0]

    def start(r, carry):
        _row_copy(ys_hbm, pos_ref[0, 0, r], buf_ref, r, sem).start()
        return carry

    def wait(r, carry):
        _row_copy(ys_hbm, 0, buf_ref, r, sem).wait()
        return carry

    lax.fori_loop(0, MOE_TOPK * tc, start, 0, unroll=8)
    lax.fori_loop(0, MOE_TOPK * tc, wait, 0, unroll=8)
    h = x_ref[...]
    for k in range(MOE_TOPK):
        h = h + r_ref[:, MOE_TOPK + k:MOE_TOPK + k + 1] * buf_ref[pl.ds(k * tc, tc), :]
    o_ref[...] = _rmsnorm(h, g_ref[...])


def combine(x, route_info, pos, ys, g):
    n, d = x.shape
    tc = pos.shape[2] // MOE_TOPK
    vmem = 4 * tc * d * 4 + MOE_TOPK * tc * d * 4 + 2 * tc * LANES * 4 + 3 * tc * d * 4
    return pl.pallas_call(
        _combine_kernel,
        grid=(n // tc,),
        in_specs=[pl.BlockSpec((1, 1, MOE_TOPK * tc), lambda i: (i, 0, 0), memory_space=pltpu.SMEM),
                  pl.BlockSpec((tc, d), lambda i: (i, 0)),
                  pl.BlockSpec((tc, LANES), lambda i: (i, 0)),
                  pl.BlockSpec((1, d), lambda i: (0, 0)),
                  pl.BlockSpec(memory_space=pl.ANY)],
        out_specs=pl.BlockSpec((tc, d), lambda i: (i, 0)),
        out_shape=jax.ShapeDtypeStruct((n, d), F32),
        scratch_shapes=[pltpu.VMEM((MOE_TOPK * tc, d), F32), pltpu.SemaphoreType.DMA(())],
        compiler_params=_params(("arbitrary",), vmem + (8 << 20)),
        name="combine",
    )(pos, x, route_info, g.reshape(1, d), ys)


def _routing_plan(experts, n_experts, tm):
    n = experts.shape[0]
    flat = experts.T.reshape(-1)
    onehot = (flat[:, None] == jnp.arange(n_experts, dtype=jnp.int32)[None, :]).astype(jnp.int32)
    csum = jnp.cumsum(onehot, axis=0)
    counts = csum[-1]
    rank = jnp.take_along_axis(csum, flat[:, None], axis=1)[:, 0] - 1
    padded = (counts + tm - 1) // tm * tm
    gend = jnp.cumsum(padded)
    pos = (gend - padded)[flat] + rank
    p_rows = MOE_TOPK * n + n_experts * tm
    tokens = jnp.tile(jnp.arange(n, dtype=jnp.int32), MOE_TOPK)
    src = jnp.zeros((p_rows,), jnp.int32).at[pos].set(tokens)
    tile_start = jnp.arange(p_rows // tm, dtype=jnp.int32) * tm
    tile_valid = (tile_start < gend[-1]).astype(jnp.int32)
    tile_expert = jnp.searchsorted(gend, tile_start, side="right").astype(jnp.int32)
    last_expert = jnp.max(jnp.where(counts > 0, jnp.arange(n_experts, dtype=jnp.int32), 0))
    tile_expert = jnp.where(tile_valid == 1, tile_expert, last_expert)
    return src, pos.astype(jnp.int32), tile_expert, tile_valid


def moe_block(h, g_ffn, w_router, w1, w3, w2, g_final):
    n, d = h.shape
    n_experts = w_router.shape[1]
    tm = _tile(MOE_TOPK * n, 512)
    xn, route_info = route(h, g_ffn, w_router)
    experts = route_info[:, :MOE_TOPK].astype(jnp.int32)
    src, pos, tile_expert, tile_valid = _routing_plan(experts, n_experts, tm)
    xs = gather_rows(xn, src, _tile(src.shape[0], 512), BF16)
    ys = expert_ffn(xs, tile_expert, tile_valid, w1, w3, w2, tm)
    tc = _tile(n, 256)
    pos_tiles = jnp.stack([pos[k * n:(k + 1) * n].reshape(n // tc, tc) for k in range(MOE_TOPK)], axis=1)
    return combine(h, route_info, pos_tiles.reshape(n // tc, 1, MOE_TOPK * tc), ys, g_final)


def kernel(x, attn_norm_g, ffn_norm_g, a_w_qkv, a_w_o, kv_norm_g, b_w_kv, b_w_q, b_w_o,
           dense_w1, dense_w3, dense_w2, moe_w_router, moe_w1, moe_w3, moe_w2, final_norm_g):
    assert a_w_qkv.shape[0] == 1 and b_w_q.shape[0] == 1 and dense_w1.shape[0] == 1 and moe_w1.shape[0] == 1
    batch, seq, d = x.shape
    n = batch * seq
    heads_a = a_w_qkv.shape[2] // (3 * HEAD_DIM)
    heads_b = b_w_q.shape[2] // HEAD_DIM
    bf = lambda w: w.astype(BF16)
    h = x.reshape(n, d)

    w_qk, w_v = a_w_qkv[0][:, :2 * heads_a * HEAD_DIM], a_w_qkv[0][:, 2 * heads_a * HEAD_DIM:]
    qk = norm_matmul_heads(h, attn_norm_g[0], bf(w_qk))
    vt = norm_matmul_t(h, attn_norm_g[0], bf(w_v.T))
    mix = moba_attention(qk, vt, batch, seq, heads_a)
    h = matmul_res(mix, bf(a_w_o[0]), h)
    h = dense_ffn(h, ffn_norm_g[0], bf(dense_w1[0]), bf(dense_w3[0]), bf(dense_w2[0]))

    kv = norm_matmul_heads(h, kv_norm_g, bf(b_w_kv))
    q = norm_matmul_heads(h, attn_norm_g[1], bf(b_w_q[0]))
    mix = stickbreak_attention(q, kv, batch, seq, heads_b)
    h = matmul_res(mix, bf(b_w_o[0]), h)
    out = moe_block(h, ffn_norm_g[1], moe_w_router[0], bf(moe_w1[0]), bf(moe_w3[0]), bf(moe_w2[0]),
                    final_norm_g)
    return out.reshape(batch, seq, d)
```

```python
import functools

import jax
import jax.numpy as jnp
from jax import lax
from jax.experimental import pallas as pl
from jax.experimental.pallas import tpu as pltpu

HEAD_DIM = 128
MOBA_BLOCK = 256
MOBA_TOPK = 3
MOE_TOPK = 2
RMS_EPS = 1e-6
NEG_INF = -1e30
LANES = 128
VMEM_CAP = 60 * 1024 * 1024
F32_EXP2_ZERO_BELOW = -150.0

F32 = jnp.float32
BF16 = jnp.bfloat16


def _tile(dim, pref):
    t = min(dim, pref)
    while dim % t:
        t //= 2
    return t


def _params(sem, vmem_bytes):
    return pltpu.CompilerParams(dimension_semantics=sem,
                                vmem_limit_bytes=int(min(VMEM_CAP, max(vmem_bytes, 16 * 1024 * 1024))))


def _rmsnorm(x, g):
    return x * lax.rsqrt(jnp.mean(x * x, axis=-1, keepdims=True) + RMS_EPS) * g


def _norm_matmul_kernel(x_ref, g_ref, w_ref, o_ref, xn_ref):
    @pl.when(pl.program_id(1) == 0)
    def _():
        xn_ref[...] = _rmsnorm(x_ref[...], g_ref[...]).astype(BF16)

    acc = jnp.dot(xn_ref[...], w_ref[...], preferred_element_type=F32)
    for c in range(o_ref.shape[0]):
        o_ref[c] = acc[:, c * HEAD_DIM:(c + 1) * HEAD_DIM].astype(o_ref.dtype)


def norm_matmul_heads(x, g, w):
    n, d = x.shape
    dout = w.shape[1]
    tm, tn = _tile(n, 1024), _tile(dout, 512)
    vmem = 2 * tm * d * 4 + tm * d * 2 + 2 * d * tn * 2 + 2 * tm * tn * 2 + 2 * tm * tn * 4
    return pl.pallas_call(
        _norm_matmul_kernel,
        grid=(n // tm, dout // tn),
        in_specs=[pl.BlockSpec((tm, d), lambda i, j: (i, 0)),
                  pl.BlockSpec((1, d), lambda i, j: (0, 0)),
                  pl.BlockSpec((d, tn), lambda i, j: (0, j))],
        out_specs=pl.BlockSpec((tn // HEAD_DIM, tm, HEAD_DIM), lambda i, j: (j, i, 0)),
        out_shape=jax.ShapeDtypeStruct((dout // HEAD_DIM, n, HEAD_DIM), BF16),
        scratch_shapes=[pltpu.VMEM((tm, d), BF16)],
        compiler_params=_params(("arbitrary", "arbitrary"), vmem + (8 << 20)),
        name="norm_matmul",
    )(x, g.reshape(1, d), w)


def _norm_matmul_t_kernel(x_ref, g_ref, wt_ref, o_ref, xn_ref):
    @pl.when(pl.program_id(1) == 0)
    def _():
        xn_ref[...] = _rmsnorm(x_ref[...], g_ref[...]).astype(BF16)

    o_ref[...] = _nt_dot(wt_ref[...], xn_ref[...]).astype(o_ref.dtype)


def norm_matmul_t(x, g, wt):
    n, d = x.shape
    dout = wt.shape[0]
    tm, tn = _tile(n, 1024), _tile(dout, 512)
    vmem = 2 * tm * d * 4 + tm * d * 2 + 2 * d * tn * 2 + 2 * tm * tn * 2 + 2 * tm * tn * 4
    return pl.pallas_call(
        _norm_matmul_t_kernel,
        grid=(n // tm, dout // tn),
        in_specs=[pl.BlockSpec((tm, d), lambda i, j: (i, 0)),
                  pl.BlockSpec((1, d), lambda i, j: (0, 0)),
                  pl.BlockSpec((tn, d), lambda i, j: (j, 0))],
        out_specs=pl.BlockSpec((tn, tm), lambda i, j: (j, i)),
        out_shape=jax.ShapeDtypeStruct((dout, n), BF16),
        scratch_shapes=[pltpu.VMEM((tm, d), BF16)],
        compiler_params=_params(("arbitrary", "arbitrary"), vmem + (8 << 20)),
        name="norm_matmul_t",
    )(x, g.reshape(1, d), wt)


def _matmul_res_kernel(a_ref, w_ref, r_ref, o_ref):
    o_ref[...] = r_ref[...] + jnp.dot(a_ref[...], w_ref[...], preferred_element_type=F32)


def matmul_res(a, w, res):
    n, k = a.shape
    dout = w.shape[1]
    tm, tn = _tile(n, 1024), _tile(dout, 512)
    vmem = 2 * (tm * k * 2 + k * tn * 2 + 2 * tm * tn * 4) + tm * tn * 4
    return pl.pallas_call(
        _matmul_res_kernel,
        grid=(n // tm, dout // tn),
        in_specs=[pl.BlockSpec((tm, k), lambda i, j: (i, 0)),
                  pl.BlockSpec((k, tn), lambda i, j: (0, j)),
                  pl.BlockSpec((tm, tn), lambda i, j: (i, j))],
        out_specs=pl.BlockSpec((tm, tn), lambda i, j: (i, j)),
        out_shape=jax.ShapeDtypeStruct((n, dout), F32),
        compiler_params=_params(("arbitrary", "arbitrary"), vmem + (8 << 20)),
        name="matmul_res",
    )(a, w, res)


def _swiglu_act(xn, w1, w3):
    gate = jnp.dot(xn, w1, preferred_element_type=F32)
    up = jnp.dot(xn, w3, preferred_element_type=F32)
    return (gate / (1.0 + jnp.exp(-gate)) * up).astype(BF16)


def _dense_ffn_kernel(x_ref, g_ref, w1_ref, w3_ref, w2_ref, o_ref, xn_ref, acc_ref):
    f = pl.program_id(1)

    @pl.when(f == 0)
    def _():
        xn_ref[...] = _rmsnorm(x_ref[...], g_ref[...]).astype(BF16)
        acc_ref[...] = jnp.zeros_like(acc_ref)

    act = _swiglu_act(xn_ref[...], w1_ref[...], w3_ref[...])
    acc_ref[...] += jnp.dot(act, w2_ref[...], preferred_element_type=F32)

    @pl.when(f == pl.num_programs(1) - 1)
    def _():
        o_ref[...] = x_ref[...] + acc_ref[...]


def dense_ffn(x, g, w1, w3, w2):
    n, d = x.shape
    dff = w1.shape[1]
    tm, tf = _tile(n, 512), _tile(dff, 512)
    vmem = 4 * tm * d * 4 + tm * d * 2 + tm * d * 4 + 2 * 3 * d * tf * 2 + 3 * tm * tf * 4
    return pl.pallas_call(
        _dense_ffn_kernel,
        grid=(n // tm, dff // tf),
        in_specs=[pl.BlockSpec((tm, d), lambda i, f: (i, 0)),
                  pl.BlockSpec((1, d), lambda i, f: (0, 0)),
                  pl.BlockSpec((d, tf), lambda i, f: (0, f)),
                  pl.BlockSpec((d, tf), lambda i, f: (0, f)),
                  pl.BlockSpec((tf, d), lambda i, f: (f, 0))],
        out_specs=pl.BlockSpec((tm, d), lambda i, f: (i, 0)),
        out_shape=jax.ShapeDtypeStruct((n, d), F32),
        scratch_shapes=[pltpu.VMEM((tm, d), BF16), pltpu.VMEM((tm, d), F32)],
        compiler_params=_params(("arbitrary", "arbitrary"), vmem + (8 << 20)),
        name="dense_ffn",
    )(x, g.reshape(1, d), w1, w3, w2)


def _nt_dot(a, b):
    return lax.dot_general(a, b, (((1,), (1,)), ((), ())), preferred_element_type=F32)


LOG2E = 1.4426950408889634
PAST_TILE = 2


def _moba_kernel(slope_ref, q_ref, k_ref, vt_ref, o_ref,
                 kmh_ref, kml_ref, pick_ref, bias_ref, m_ref, l_ref, acc_ref, sa_ref, sb_ref, *, n_blocks):
    h = pl.program_id(1)
    i = pl.program_id(2)
    blk = MOBA_BLOCK
    slope2 = slope_ref[h] * LOG2E
    scale2 = HEAD_DIM ** -0.5 * LOG2E

    @pl.when(i == 0)
    def _():
        kmh_ref[...] = jnp.zeros_like(kmh_ref)
        kml_ref[...] = jnp.zeros_like(kml_ref)
        pick_ref[...] = jnp.zeros_like(pick_ref)

        def mean_block(n, carry):
            kb = k_ref[0, pl.ds(pl.multiple_of(n * blk, blk), blk), :].astype(F32)
            km = jnp.sum(kb, axis=0, keepdims=True) * (1.0 / blk)
            hi = km.astype(BF16).astype(F32)
            kmh_ref[pl.ds(n, 1), :] = hi
            kml_ref[pl.ds(n, 1), :] = km - hi
            return carry

        lax.fori_loop(0, n_blocks, mean_block, 0)
        key = lax.broadcasted_iota(jnp.int32, bias_ref.shape, 0)
        qry = lax.broadcasted_iota(jnp.int32, bias_ref.shape, 1)
        bias_ref[...] = slope2 * (qry - key).astype(F32)

    q = q_ref[0]
    nbp = kmh_ref.shape[0]
    gate = _nt_dot(kmh_ref[...].astype(BF16), q) + _nt_dot(kml_ref[...].astype(BF16), q)
    sub = lax.broadcasted_iota(jnp.int32, (nbp, blk), 0)
    gate = jnp.where(sub < i, gate, NEG_INF)
    pick = jnp.zeros((nbp, blk), F32)
    for kk in range(min(MOBA_TOPK, n_blocks)):
        mx = jnp.max(gate, axis=0, keepdims=True)
        idx = jnp.min(jnp.where(gate == mx, sub, nbp), axis=0, keepdims=True)
        pick = jnp.where(sub == jnp.where(kk < i, idx, -1), 1.0, pick)
        gate = jnp.where(sub == idx, -jnp.inf, gate)
    pick_ref[pl.ds(0, nbp), :] = pick

    base = pl.multiple_of(i * blk, blk)
    key = lax.broadcasted_iota(jnp.int32, (blk, blk), 0)
    qry = lax.broadcasted_iota(jnp.int32, (blk, blk), 1)
    s = _nt_dot(k_ref[0, pl.ds(base, blk), :], q) * scale2 - bias_ref[pl.ds(0, blk), :]
    s = jnp.where(qry >= key, s, NEG_INF)
    m0 = jnp.max(s, axis=0, keepdims=True)
    p = jnp.exp2(s - m0)
    m_ref[...] = m0
    l_ref[...] = jnp.sum(p, axis=0, keepdims=True)
    acc_ref[...] = jnp.dot(vt_ref[:, pl.ds(base, blk)], p.astype(BF16), preferred_element_type=F32)

    width = PAST_TILE * blk
    n_tiles = (i + PAST_TILE - 1) // PAST_TILE

    def window(t):
        return pl.ds(pl.multiple_of(jnp.minimum(t, n_blocks // PAST_TILE - 1) * width, width), width)

    def raw_scores(t):
        return _nt_dot(k_ref[0, window(t), :], q)

    def past_tile(t, raw):
        j0 = t * PAST_TILE
        sj = raw * scale2 - bias_ref[...]
        parts = []
        for u in range(PAST_TILE):
            picked = pick_ref[pl.ds(j0 + u, 1), :] > 0.5
            parts.append(jnp.where(picked, sj[u * blk:(u + 1) * blk], NEG_INF))
        sj = jnp.concatenate(parts, axis=0)
        shift = slope2 * ((i - j0) * blk).astype(F32)
        m_old = m_ref[...]
        m_new = jnp.maximum(m_old, jnp.max(sj, axis=0, keepdims=True) - shift)
        pj = jnp.exp2(sj - (m_new + shift))
        alpha = jnp.exp2(m_old - m_new)
        m_ref[...] = m_new
        l_ref[...] = alpha * l_ref[...] + jnp.sum(pj, axis=0, keepdims=True)
        acc_ref[...] = alpha * acc_ref[...] + jnp.dot(
            vt_ref[:, window(t)], pj.astype(BF16), preferred_element_type=F32)

    sa_ref[...] = raw_scores(0)

    def tile_pair(u, carry):
        sb_ref[...] = raw_scores(2 * u + 1)
        past_tile(2 * u, sa_ref[...])
        sa_ref[...] = raw_scores(2 * u + 2)
        past_tile(2 * u + 1, sb_ref[...])
        return carry

    lax.fori_loop(0, (n_tiles + 1) // 2, tile_pair, 0)
    o_ref[...] = (acc_ref[...] / l_ref[...]).T.astype(o_ref.dtype)


def moba_attention(qk, vt, batch, seq, n_heads):
    assert seq % (PAST_TILE * MOBA_BLOCK) == 0
    n = batch * seq
    nqb = seq // MOBA_BLOCK
    nbp = -(-nqb // 8) * 8
    slopes = jnp.asarray([2.0 ** (-8.0 * (h + 1) / n_heads) for h in range(n_heads)], dtype=F32)
    blk = MOBA_BLOCK
    vmem = 2 * (2 * seq * HEAD_DIM * 2) + 4 * blk * HEAD_DIM * 2 + 11 * PAST_TILE * blk * blk * 4
    grid_spec = pltpu.PrefetchScalarGridSpec(
        num_scalar_prefetch=1,
        grid=(batch, n_heads, nqb),
        in_specs=[pl.BlockSpec((1, blk, HEAD_DIM), lambda b, h, i, s: (h, b * nqb + i, 0)),
                  pl.BlockSpec((1, seq, HEAD_DIM), lambda b, h, i, s: (n_heads + h, b, 0)),
                  pl.BlockSpec((HEAD_DIM, seq), lambda b, h, i, s: (h, b))],
        out_specs=pl.BlockSpec((blk, HEAD_DIM), lambda b, h, i, s: (b * nqb + i, h)),
        scratch_shapes=[pltpu.VMEM((nbp, HEAD_DIM), F32), pltpu.VMEM((nbp, HEAD_DIM), F32),
                        pltpu.VMEM((nbp + PAST_TILE, blk), F32), pltpu.VMEM((PAST_TILE * blk, blk), F32),
                        pltpu.VMEM((1, blk), F32), pltpu.VMEM((1, blk), F32),
                        pltpu.VMEM((HEAD_DIM, blk), F32),
                        pltpu.VMEM((PAST_TILE * blk, blk), F32), pltpu.VMEM((PAST_TILE * blk, blk), F32)],
    )
    return pl.pallas_call(
        functools.partial(_moba_kernel, n_blocks=nqb),
        grid_spec=grid_spec,
        out_shape=jax.ShapeDtypeStruct((n, n_heads * HEAD_DIM), BF16),
        compiler_params=_params(("arbitrary", "arbitrary", "arbitrary"), vmem + (8 << 20)),
        name="moba",
    )(slopes, qk, qk, vt)


SB_BLOCK = 256


def _stickbreak_kernel(q_ref, k_ref, v_ref, o_ref, carry_ref, acc_ref):
    i = pl.program_id(2)
    t = SB_BLOCK
    scale = HEAD_DIM ** -0.5 * LOG2E
    q = q_ref[0]
    row = lax.broadcasted_iota(jnp.int32, (t, t), 0)
    col = lax.broadcasted_iota(jnp.int32, (t, t), 1)
    later = (row > col).astype(BF16)

    def log_sigmoid(z):
        return jnp.minimum(z, 0.0) - jnp.log2(1.0 + jnp.exp2(-jnp.abs(z)))

    def suffix_in_block(log_keep):
        hi = log_keep.astype(BF16)
        lo = (log_keep - hi.astype(F32)).astype(BF16)
        return jnp.dot(hi, later, preferred_element_type=F32) + jnp.dot(lo, later, preferred_element_type=F32)

    first = jnp.maximum(i - 1, 0)
    kbase = pl.multiple_of(first * t, t)
    z = _nt_dot(q, k_ref[0, pl.ds(kbase, 2 * t), :]) * scale
    log_beta = log_sigmoid(z)
    qrow = lax.broadcasted_iota(jnp.int32, (t, 2 * t), 0)
    kcol = lax.broadcasted_iota(jnp.int32, (t, 2 * t), 1)
    causal = kcol + (first - i) * t < qrow
    log_keep = jnp.where(causal, log_beta - z, 0.0)
    tail_sum = jnp.sum(log_keep[:, t:], axis=-1, keepdims=True)
    suffix = suffix_in_block(jnp.concatenate([log_keep[:, :t], log_keep[:, t:]], axis=0))
    suffix = jnp.concatenate([suffix[:t] + tail_sum, suffix[t:]], axis=1)
    w = jnp.where(causal, jnp.exp2(log_beta + suffix), 0.0)
    acc_ref[...] = jnp.dot(w.astype(BF16), v_ref[0, pl.ds(kbase, 2 * t), :], preferred_element_type=F32)
    carry_ref[...] = jnp.sum(log_keep[:, :t], axis=-1, keepdims=True) + tail_sum

    def key_block(jb):
        base = pl.multiple_of(jb * t, t)
        zb = _nt_dot(q, k_ref[0, pl.ds(base, t), :]) * scale
        lb = log_sigmoid(zb)
        lk = lb - zb
        wb = jnp.exp2(lb + suffix_in_block(lk) + carry_ref[...])
        acc_ref[...] += jnp.dot(wb.astype(BF16), v_ref[0, pl.ds(base, t), :], preferred_element_type=F32)
        carry_ref[...] += jnp.sum(lk, axis=-1, keepdims=True)

    def cond(state):
        jb, worst = state
        return jnp.logical_and(jb >= 0, worst > F32_EXP2_ZERO_BELOW)

    def body(state):
        jb, _ = state
        key_block(jb)
        return jb - 1, jnp.max(carry_ref[...])

    lax.while_loop(cond, body, (first - 1, jnp.max(carry_ref[...])))
    o_ref[...] = acc_ref[...].astype(o_ref.dtype)


def stickbreak_attention(q, kv, batch, seq, n_heads):
    assert seq % SB_BLOCK == 0 and seq >= 2 * SB_BLOCK
    n = batch * seq
    nqb = seq // SB_BLOCK
    t = SB_BLOCK
    vmem = 2 * (2 * seq * HEAD_DIM * 2) + 4 * t * HEAD_DIM * 2 + 14 * t * t * 4
    return pl.pallas_call(
        _stickbreak_kernel,
        grid=(batch, n_heads, nqb),
        in_specs=[pl.BlockSpec((1, t, HEAD_DIM), lambda b, h, i: (h, b * nqb + i, 0)),
                  pl.BlockSpec((1, seq, HEAD_DIM), lambda b, h, i: (h, b, 0)),
                  pl.BlockSpec((1, seq, HEAD_DIM), lambda b, h, i: (n_heads + h, b, 0))],
        out_specs=pl.BlockSpec((t, HEAD_DIM), lambda b, h, i: (b * nqb + i, h)),
        out_shape=jax.ShapeDtypeStruct((n, n_heads * HEAD_DIM), BF16),
        scratch_shapes=[pltpu.VMEM((t, 1), F32), pltpu.VMEM((t, HEAD_DIM), F32)],
        compiler_params=_params(("arbitrary", "arbitrary", "arbitrary"), vmem + (8 << 20)),
        name="stickbreak",
    )(q, kv, kv)


def _route_kernel(x_ref, g_ref, wh_ref, wl_ref, xn_ref, r_ref, *, n_experts):
    xn = _rmsnorm(x_ref[...], g_ref[...])
    xn_ref[...] = xn
    hi = xn.astype(BF16)
    lo = (xn - hi.astype(F32)).astype(BF16)
    wh, wl = wh_ref[...], wl_ref[...]
    logits = (jnp.dot(hi, wh, preferred_element_type=F32) + jnp.dot(hi, wl, preferred_element_type=F32)
              + jnp.dot(lo, wh, preferred_element_type=F32))
    lane = lax.broadcasted_iota(jnp.int32, logits.shape, 1)
    logits = jnp.where(lane < n_experts, logits, -jnp.inf)
    v1 = jnp.max(logits, axis=-1, keepdims=True)
    i1 = jnp.min(jnp.where(logits == v1, lane, LANES), axis=-1, keepdims=True)
    rest = jnp.where(lane == i1, -jnp.inf, logits)
    v2 = jnp.max(rest, axis=-1, keepdims=True)
    i2 = jnp.min(jnp.where(rest == v2, lane, LANES), axis=-1, keepdims=True)
    e2 = jnp.exp(v2 - v1)
    w1 = 1.0 / (1.0 + e2)
    w2 = e2 / (1.0 + e2)
    out = jnp.where(lane == 0, i1.astype(F32),
                    jnp.where(lane == 1, i2.astype(F32),
                              jnp.where(lane == 2, w1, jnp.where(lane == 3, w2, 0.0))))
    r_ref[...] = out


def route(x, g, w_router):
    n, d = x.shape
    n_experts = w_router.shape[1]
    assert n_experts <= LANES
    wpad = jnp.zeros((d, LANES), F32).at[:, :n_experts].set(w_router)
    wh = wpad.astype(BF16)
    wl = (wpad - wh.astype(F32)).astype(BF16)
    tm = _tile(n, 512)
    vmem = 4 * tm * d * 4 + 4 * tm * d * 4 + 4 * d * LANES * 2 + 4 * tm * LANES * 4
    return pl.pallas_call(
        functools.partial(_route_kernel, n_experts=n_experts),
        grid=(n // tm,),
        in_specs=[pl.BlockSpec((tm, d), lambda i: (i, 0)),
                  pl.BlockSpec((1, d), lambda i: (0, 0)),
                  pl.BlockSpec((d, LANES), lambda i: (0, 0)),
                  pl.BlockSpec((d, LANES), lambda i: (0, 0))],
        out_specs=[pl.BlockSpec((tm, d), lambda i: (i, 0)),
                   pl.BlockSpec((tm, LANES), lambda i: (i, 0))],
        out_shape=[jax.ShapeDtypeStruct((n, d), F32), jax.ShapeDtypeStruct((n, LANES), F32)],
        compiler_params=_params(("arbitrary",), vmem + (8 << 20)),
        name="route",
    )(x, g.reshape(1, d), wh, wl)


def _row_copy(src_hbm, idx, dst_ref, r, sem):
    return pltpu.make_async_copy(src_hbm.at[pl.ds(idx, 1), :], dst_ref.at[pl.ds(r, 1), :], sem)


def _ring_gather(idx_ref, idx_next_ref, src_hbm, buf_ref, sem):
    s = pl.program_id(0)
    rows = buf_ref.shape[1]

    def issue(indices, slot):
        def start(r, carry):
            _row_copy(src_hbm, indices[0, 0, r], buf_ref.at[slot], r, sem.at[slot]).start()
            return carry

        lax.fori_loop(0, rows, start, 0, unroll=8)

    @pl.when(s == 0)
    def _():
        issue(idx_ref, 0)

    @pl.when(s + 1 < pl.num_programs(0))
    def _():
        issue(idx_next_ref, (s + 1) % 2)

    slot = s % 2

    def wait(r, carry):
        _row_copy(src_hbm, 0, buf_ref.at[slot], r, sem.at[slot]).wait()
        return carry

    lax.fori_loop(0, rows, wait, 0, unroll=8)
    return buf_ref.at[slot]


def _ring_specs(n_steps, rows):
    return [pl.BlockSpec((1, 1, rows), lambda i: (i, 0, 0), memory_space=pltpu.SMEM),
            pl.BlockSpec((1, 1, rows), lambda i: (jnp.minimum(i + 1, n_steps - 1), 0, 0),
                         memory_space=pltpu.SMEM)]


def _gather_rows_kernel(idx_ref, idx_next_ref, src_hbm, o_ref, buf_ref, sem):
    o_ref[...] = _ring_gather(idx_ref, idx_next_ref, src_hbm, buf_ref, sem)[...].astype(o_ref.dtype)


def gather_rows(src, idx, rows_per_step, out_dtype):
    n_out = idx.shape[0]
    d = src.shape[1]
    r = rows_per_step
    assert n_out % r == 0
    n_steps = n_out // r
    idx = idx.reshape(n_steps, 1, r)
    vmem = 2 * r * d * 4 + 2 * r * d * 4
    return pl.pallas_call(
        _gather_rows_kernel,
        grid=(n_steps,),
        in_specs=_ring_specs(n_steps, r) + [pl.BlockSpec(memory_space=pl.ANY)],
        out_specs=pl.BlockSpec((r, d), lambda i: (i, 0)),
        out_shape=jax.ShapeDtypeStruct((n_out, d), out_dtype),
        scratch_shapes=[pltpu.VMEM((2, r, d), src.dtype), pltpu.SemaphoreType.DMA((2,))],
        compiler_params=_params(("arbitrary",), vmem + (8 << 20)),
        name="gather_rows",
    )(idx, idx, src)


def _expert_ffn_kernel(texp_ref, tvalid_ref, xs_ref, w1_ref, w3_ref, w2_ref, o_ref):
    t = pl.program_id(0)
    f = pl.program_id(1)

    @pl.when(f == 0)
    def _():
        o_ref[...] = jnp.zeros_like(o_ref)

    @pl.when(tvalid_ref[t] == 1)
    def _():
        act = _swiglu_act(xs_ref[...], w1_ref[0].astype(BF16), w3_ref[0].astype(BF16))
        o_ref[...] += jnp.dot(act, w2_ref[0].astype(BF16), preferred_element_type=F32)


def expert_ffn(xs, tile_expert, tile_valid, w1, w3, w2, tm):
    p, d = xs.shape
    dff = w1.shape[2]
    wbytes = w1.dtype.itemsize
    tf = _tile(dff, 1024 // wbytes)
    nf = dff // tf
    n_tiles = p // tm

    def wcol(t, f, te, tv):
        return (te[t], 0, jnp.where(tv[t] == 1, f, nf - 1))

    def wrow(t, f, te, tv):
        return (te[t], jnp.where(tv[t] == 1, f, nf - 1), 0)

    vmem = (2 * tm * d * 2 + 2 * tm * d * 4 + 2 * 3 * d * tf * wbytes + 3 * d * tf * 2
            + 3 * tm * tf * 4 + tm * d * 4)
    grid_spec = pltpu.PrefetchScalarGridSpec(
        num_scalar_prefetch=2,
        grid=(n_tiles, nf),
        in_specs=[pl.BlockSpec((tm, d), lambda t, f, te, tv: (t, 0)),
                  pl.BlockSpec((1, d, tf), wcol),
                  pl.BlockSpec((1, d, tf), wcol),
                  pl.BlockSpec((1, tf, d), wrow)],
        out_specs=pl.BlockSpec((tm, d), lambda t, f, te, tv: (t, 0)),
    )
    return pl.pallas_call(
        _expert_ffn_kernel,
        grid_spec=grid_spec,
        out_shape=jax.ShapeDtypeStruct((p, d), F32),
        compiler_params=_params(("arbitrary", "arbitrary"), vmem + (8 << 20)),
        name="expert_ffn",
    )(tile_expert, tile_valid, xs, w1, w3, w2)


def _combine_kernel(pos_ref, pos_next_ref, x_ref, r_ref, g_ref, ys_hbm, o_ref, buf_ref, sem):
    tc = x_ref.shape[0]
    rows = _ring_gather(pos_ref, pos_next_ref, ys_hbm, buf_ref, sem)
    h = x_ref[...]
    for k in range(MOE_TOPK):
        h = h + r_ref[:, MOE_TOPK + k:MOE_TOPK + k + 1] * rows[pl.ds(k * tc, tc), :]
    o_ref[...] = _rmsnorm(h, g_ref[...])


def combine(x, route_info, pos, ys, g):
    n, d = x.shape
    tc = pos.shape[2] // MOE_TOPK
    vmem = 4 * tc * d * 4 + 2 * MOE_TOPK * tc * d * 4 + 2 * tc * LANES * 4 + 3 * tc * d * 4
    return pl.pallas_call(
        _combine_kernel,
        grid=(n // tc,),
        in_specs=_ring_specs(n // tc, MOE_TOPK * tc) + [
                  pl.BlockSpec((tc, d), lambda i: (i, 0)),
                  pl.BlockSpec((tc, LANES), lambda i: (i, 0)),
                  pl.BlockSpec((1, d), lambda i: (0, 0)),
                  pl.BlockSpec(memory_space=pl.ANY)],
        out_specs=pl.BlockSpec((tc, d), lambda i: (i, 0)),
        out_shape=jax.ShapeDtypeStruct((n, d), F32),
        scratch_shapes=[pltpu.VMEM((2, MOE_TOPK * tc, d), F32), pltpu.SemaphoreType.DMA((2,))],
        compiler_params=_params(("arbitrary",), vmem + (8 << 20)),
        name="combine",
    )(pos, pos, x, route_info, g.reshape(1, d), ys)


def _routing_plan(experts, n_experts, tm):
    n = experts.shape[0]
    flat = experts.T.reshape(-1)
    onehot = (flat[:, None] == jnp.arange(n_experts, dtype=jnp.int32)[None, :]).astype(jnp.int32)
    csum = jnp.cumsum(onehot, axis=0)
    counts = csum[-1]
    rank = jnp.take_along_axis(csum, flat[:, None], axis=1)[:, 0] - 1
    padded = (counts + tm - 1) // tm * tm
    gend = jnp.cumsum(padded)
    pos = (gend - padded)[flat] + rank
    p_rows = MOE_TOPK * n + n_experts * tm
    tokens = jnp.tile(jnp.arange(n, dtype=jnp.int32), MOE_TOPK)
    src = jnp.zeros((p_rows,), jnp.int32).at[pos].set(tokens)
    tile_start = jnp.arange(p_rows // tm, dtype=jnp.int32) * tm
    tile_valid = (tile_start < gend[-1]).astype(jnp.int32)
    tile_expert = jnp.sum((gend[None, :] <= tile_start[:, None]).astype(jnp.int32), axis=1)
    last_expert = jnp.max(jnp.where(counts > 0, jnp.arange(n_experts, dtype=jnp.int32), 0))
    tile_expert = jnp.where(tile_valid == 1, tile_expert, last_expert)
    return src, pos.astype(jnp.int32), tile_expert, tile_valid


def moe_block(h, g_ffn, w_router, w1, w3, w2, g_final):
    n, d = h.shape
    n_experts = w_router.shape[1]
    tm = _tile(MOE_TOPK * n, 1024)
    xn, route_info = route(h, g_ffn, w_router)
    experts = route_info[:, :MOE_TOPK].astype(jnp.int32)
    src, pos, tile_expert, tile_valid = _routing_plan(experts, n_experts, tm)
    xs = gather_rows(xn, src, _tile(src.shape[0], 512), BF16)
    ys = expert_ffn(xs, tile_expert, tile_valid, w1, w3, w2, tm)
    tc = _tile(n, 256)
    pos_tiles = jnp.stack([pos[k * n:(k + 1) * n].reshape(n // tc, tc) for k in range(MOE_TOPK)], axis=1)
    return combine(h, route_info, pos_tiles.reshape(n // tc, 1, MOE_TOPK * tc), ys, g_final)


def kernel(x, attn_norm_g, ffn_norm_g, a_w_qkv, a_w_o, kv_norm_g, b_w_kv, b_w_q, b_w_o,
           dense_w1, dense_w3, dense_w2, moe_w_router, moe_w1, moe_w3, moe_w2, final_norm_g):
    assert a_w_qkv.shape[0] == 1 and b_w_q.shape[0] == 1 and dense_w1.shape[0] == 1 and moe_w1.shape[0] == 1
    batch, seq, d = x.shape
    n = batch * seq
    heads_a = a_w_qkv.shape[2] // (3 * HEAD_DIM)
    heads_b = b_w_q.shape[2] // HEAD_DIM
    bf = lambda w: w.astype(BF16)
    h = x.reshape(n, d)

    w_qk, w_v = a_w_qkv[0][:, :2 * heads_a * HEAD_DIM], a_w_qkv[0][:, 2 * heads_a * HEAD_DIM:]
    qk = norm_matmul_heads(h, attn_norm_g[0], bf(w_qk))
    vt = norm_matmul_t(h, attn_norm_g[0], bf(w_v.T))
    mix = moba_attention(qk, vt, batch, seq, heads_a)
    h = matmul_res(mix, bf(a_w_o[0]), h)
    h = dense_ffn(h, ffn_norm_g[0], bf(dense_w1[0]), bf(dense_w3[0]), bf(dense_w2[0]))

    kv = norm_matmul_heads(h, kv_norm_g, bf(b_w_kv))
    q = norm_matmul_heads(h, attn_norm_g[1], bf(b_w_q[0]))
    mix = stickbreak_attention(q, kv, batch, seq, heads_b)
    h = matmul_res(mix, bf(b_w_o[0]), h)
    out = moe_block(h, ffn_norm_g[1], moe_w_router[0], moe_w1[0], moe_w3[0], moe_w2[0], final_norm_g)
    return out.reshape(batch, seq, d)
```

```python
import functools

import jax
import jax.numpy as jnp
from jax import lax
from jax.experimental import pallas as pl
from jax.experimental.pallas import tpu as pltpu

HEAD_DIM = 128
MOBA_BLOCK = 256
MOBA_TOPK = 3
MOE_TOPK = 2
RMS_EPS = 1e-6
NEG_INF = -1e30
LANES = 128
VMEM_CAP = 60 * 1024 * 1024
F32_EXP2_ZERO_BELOW = -150.0

F32 = jnp.float32
BF16 = jnp.bfloat16


def _tile(dim, pref):
    t = min(dim, pref)
    while dim % t:
        t //= 2
    return t


def _params(sem, vmem_bytes):
    return pltpu.CompilerParams(dimension_semantics=sem,
                                vmem_limit_bytes=int(min(VMEM_CAP, max(vmem_bytes, 16 * 1024 * 1024))))


def _rmsnorm(x, g):
    return x * lax.rsqrt(jnp.mean(x * x, axis=-1, keepdims=True) + RMS_EPS) * g


def _norm_matmul_kernel(x_ref, g_ref, w_ref, o_ref, xn_ref):
    @pl.when(pl.program_id(1) == 0)
    def _():
        xn_ref[...] = _rmsnorm(x_ref[...], g_ref[...]).astype(BF16)

    acc = jnp.dot(xn_ref[...], w_ref[...], preferred_element_type=F32)
    for c in range(o_ref.shape[0]):
        o_ref[c] = acc[:, c * HEAD_DIM:(c + 1) * HEAD_DIM].astype(o_ref.dtype)


def norm_matmul_heads(x, g, w):
    n, d = x.shape
    dout = w.shape[1]
    tm, tn = _tile(n, 1024), _tile(dout, 512)
    vmem = 2 * tm * d * 4 + tm * d * 2 + 2 * d * tn * 2 + 2 * tm * tn * 2 + 2 * tm * tn * 4
    return pl.pallas_call(
        _norm_matmul_kernel,
        grid=(n // tm, dout // tn),
        in_specs=[pl.BlockSpec((tm, d), lambda i, j: (i, 0)),
                  pl.BlockSpec((1, d), lambda i, j: (0, 0)),
                  pl.BlockSpec((d, tn), lambda i, j: (0, j))],
        out_specs=pl.BlockSpec((tn // HEAD_DIM, tm, HEAD_DIM), lambda i, j: (j, i, 0)),
        out_shape=jax.ShapeDtypeStruct((dout // HEAD_DIM, n, HEAD_DIM), BF16),
        scratch_shapes=[pltpu.VMEM((tm, d), BF16)],
        compiler_params=_params(("arbitrary", "arbitrary"), vmem + (8 << 20)),
        name="norm_matmul",
    )(x, g.reshape(1, d), w)


def _norm2_matmul_kernel(x_ref, ga_ref, gb_ref, w_ref, o_ref, xa_ref, xb_ref, *, tiles_a):
    j = pl.program_id(1)

    @pl.when(j == 0)
    def _():
        xa_ref[...] = _rmsnorm(x_ref[...], ga_ref[...]).astype(BF16)
        xb_ref[...] = _rmsnorm(x_ref[...], gb_ref[...]).astype(BF16)

    def emit(xn_ref):
        acc = jnp.dot(xn_ref[...], w_ref[...], preferred_element_type=F32)
        for c in range(o_ref.shape[0]):
            o_ref[c] = acc[:, c * HEAD_DIM:(c + 1) * HEAD_DIM].astype(o_ref.dtype)

    @pl.when(j < tiles_a)
    def _():
        emit(xa_ref)

    @pl.when(j >= tiles_a)
    def _():
        emit(xb_ref)


def norm2_matmul_heads(x, ga, wa, gb, wb):
    n, d = x.shape
    w = jnp.concatenate([wa, wb], axis=1)
    dout = w.shape[1]
    tm, tn = _tile(n, 1024), _tile(dout, 512)
    assert wa.shape[1] % tn == 0
    vmem = 2 * tm * d * 4 + 2 * tm * d * 2 + 2 * d * tn * 2 + 2 * tm * tn * 2 + 2 * tm * tn * 4
    return pl.pallas_call(
        functools.partial(_norm2_matmul_kernel, tiles_a=wa.shape[1] // tn),
        grid=(n // tm, dout // tn),
        in_specs=[pl.BlockSpec((tm, d), lambda i, j: (i, 0)),
                  pl.BlockSpec((1, d), lambda i, j: (0, 0)),
                  pl.BlockSpec((1, d), lambda i, j: (0, 0)),
                  pl.BlockSpec((d, tn), lambda i, j: (0, j))],
        out_specs=pl.BlockSpec((tn // HEAD_DIM, tm, HEAD_DIM), lambda i, j: (j, i, 0)),
        out_shape=jax.ShapeDtypeStruct((dout // HEAD_DIM, n, HEAD_DIM), BF16),
        scratch_shapes=[pltpu.VMEM((tm, d), BF16), pltpu.VMEM((tm, d), BF16)],
        compiler_params=_params(("arbitrary", "arbitrary"), vmem + (8 << 20)),
        name="norm2_matmul",
    )(x, ga.reshape(1, d), gb.reshape(1, d), w)


def _norm_matmul_t_kernel(x_ref, g_ref, wt_ref, o_ref, xn_ref):
    @pl.when(pl.program_id(1) == 0)
    def _():
        xn_ref[...] = _rmsnorm(x_ref[...], g_ref[...]).astype(BF16)

    o_ref[...] = _nt_dot(wt_ref[...], xn_ref[...]).astype(o_ref.dtype)


def norm_matmul_t(x, g, wt):
    n, d = x.shape
    dout = wt.shape[0]
    tm, tn = _tile(n, 1024), _tile(dout, 512)
    vmem = 2 * tm * d * 4 + tm * d * 2 + 2 * d * tn * 2 + 2 * tm * tn * 2 + 2 * tm * tn * 4
    return pl.pallas_call(
        _norm_matmul_t_kernel,
        grid=(n // tm, dout // tn),
        in_specs=[pl.BlockSpec((tm, d), lambda i, j: (i, 0)),
                  pl.BlockSpec((1, d), lambda i, j: (0, 0)),
                  pl.BlockSpec((tn, d), lambda i, j: (j, 0))],
        out_specs=pl.BlockSpec((tn, tm), lambda i, j: (j, i)),
        out_shape=jax.ShapeDtypeStruct((dout, n), BF16),
        scratch_shapes=[pltpu.VMEM((tm, d), BF16)],
        compiler_params=_params(("arbitrary", "arbitrary"), vmem + (8 << 20)),
        name="norm_matmul_t",
    )(x, g.reshape(1, d), wt)


def _matmul_res_kernel(a_ref, w_ref, r_ref, o_ref):
    o_ref[...] = r_ref[...] + jnp.dot(a_ref[...], w_ref[...], preferred_element_type=F32)


def matmul_res(a, w, res):
    n, k = a.shape
    dout = w.shape[1]
    tm, tn = _tile(n, 512), _tile(dout, 2048)
    vmem = 2 * (tm * k * 2 + k * tn * 2 + 2 * tm * tn * 4) + tm * tn * 4
    return pl.pallas_call(
        _matmul_res_kernel,
        grid=(n // tm, dout // tn),
        in_specs=[pl.BlockSpec((tm, k), lambda i, j: (i, 0)),
                  pl.BlockSpec((k, tn), lambda i, j: (0, j)),
                  pl.BlockSpec((tm, tn), lambda i, j: (i, j))],
        out_specs=pl.BlockSpec((tm, tn), lambda i, j: (i, j)),
        out_shape=jax.ShapeDtypeStruct((n, dout), F32),
        compiler_params=_params(("arbitrary", "arbitrary"), vmem + (8 << 20)),
        name="matmul_res",
    )(a, w, res)


def _swiglu_act(xn, w1, w3):
    gate = jnp.dot(xn, w1, preferred_element_type=F32)
    up = jnp.dot(xn, w3, preferred_element_type=F32)
    return (gate / (1.0 + jnp.exp(-gate)) * up).astype(BF16)


def _dense_ffn_kernel(x_ref, g_ref, w1_ref, w3_ref, w2_ref, o_ref, xn_ref, acc_ref):
    f = pl.program_id(1)

    @pl.when(f == 0)
    def _():
        xn_ref[...] = _rmsnorm(x_ref[...], g_ref[...]).astype(BF16)
        acc_ref[...] = jnp.zeros_like(acc_ref)

    act = _swiglu_act(xn_ref[...], w1_ref[...], w3_ref[...])
    acc_ref[...] += jnp.dot(act, w2_ref[...], preferred_element_type=F32)

    @pl.when(f == pl.num_programs(1) - 1)
    def _():
        o_ref[...] = x_ref[...] + acc_ref[...]


def dense_ffn(x, g, w1, w3, w2):
    n, d = x.shape
    dff = w1.shape[1]
    tm, tf = _tile(n, 512), _tile(dff, 512)
    vmem = 4 * tm * d * 4 + tm * d * 2 + tm * d * 4 + 2 * 3 * d * tf * 2 + 3 * tm * tf * 4
    return pl.pallas_call(
        _dense_ffn_kernel,
        grid=(n // tm, dff // tf),
        in_specs=[pl.BlockSpec((tm, d), lambda i, f: (i, 0)),
                  pl.BlockSpec((1, d), lambda i, f: (0, 0)),
                  pl.BlockSpec((d, tf), lambda i, f: (0, f)),
                  pl.BlockSpec((d, tf), lambda i, f: (0, f)),
                  pl.BlockSpec((tf, d), lambda i, f: (f, 0))],
        out_specs=pl.BlockSpec((tm, d), lambda i, f: (i, 0)),
        out_shape=jax.ShapeDtypeStruct((n, d), F32),
        scratch_shapes=[pltpu.VMEM((tm, d), BF16), pltpu.VMEM((tm, d), F32)],
        compiler_params=_params(("arbitrary", "arbitrary"), vmem + (8 << 20)),
        name="dense_ffn",
    )(x, g.reshape(1, d), w1, w3, w2)


def _nt_dot(a, b):
    return lax.dot_general(a, b, (((1,), (1,)), ((), ())), preferred_element_type=F32)


LOG2E = 1.4426950408889634
PAST_TILE = 2


def _moba_kernel(slope_ref, q_ref, k_ref, vt_ref, o_ref,
                 kmh_ref, kml_ref, pick_ref, bias_ref, m_ref, l_ref, acc_ref, sa_ref, sb_ref, *, n_blocks):
    h = pl.program_id(1)
    i = pl.program_id(2)
    blk = MOBA_BLOCK
    slope2 = slope_ref[h] * LOG2E
    scale2 = HEAD_DIM ** -0.5 * LOG2E

    @pl.when(i == 0)
    def _():
        kmh_ref[...] = jnp.zeros_like(kmh_ref)
        kml_ref[...] = jnp.zeros_like(kml_ref)
        pick_ref[...] = jnp.zeros_like(pick_ref)

        def mean_block(n, carry):
            kb = k_ref[0, pl.ds(pl.multiple_of(n * blk, blk), blk), :].astype(F32)
            km = jnp.sum(kb, axis=0, keepdims=True) * (1.0 / blk)
            hi = km.astype(BF16).astype(F32)
            kmh_ref[pl.ds(n, 1), :] = hi
            kml_ref[pl.ds(n, 1), :] = km - hi
            return carry

        lax.fori_loop(0, n_blocks, mean_block, 0)
        key = lax.broadcasted_iota(jnp.int32, bias_ref.shape, 0)
        qry = lax.broadcasted_iota(jnp.int32, bias_ref.shape, 1)
        bias_ref[...] = slope2 * (qry - key).astype(F32)

    q = q_ref[0]
    nbp = kmh_ref.shape[0]
    gate = _nt_dot(kmh_ref[...].astype(BF16), q) + _nt_dot(kml_ref[...].astype(BF16), q)
    sub = lax.broadcasted_iota(jnp.int32, (nbp, blk), 0)
    gate = jnp.where(sub < i, gate, NEG_INF)
    pick = jnp.zeros((nbp, blk), F32)
    for kk in range(min(MOBA_TOPK, n_blocks)):
        mx = jnp.max(gate, axis=0, keepdims=True)
        idx = jnp.min(jnp.where(gate == mx, sub, nbp), axis=0, keepdims=True)
        pick = jnp.where(sub == jnp.where(kk < i, idx, -1), 1.0, pick)
        gate = jnp.where(sub == idx, -jnp.inf, gate)
    pick_ref[pl.ds(0, nbp), :] = pick

    base = pl.multiple_of(i * blk, blk)
    key = lax.broadcasted_iota(jnp.int32, (blk, blk), 0)
    qry = lax.broadcasted_iota(jnp.int32, (blk, blk), 1)
    s = _nt_dot(k_ref[0, pl.ds(base, blk), :], q) * scale2 - bias_ref[pl.ds(0, blk), :]
    s = jnp.where(qry >= key, s, NEG_INF)
    m0 = jnp.max(s, axis=0, keepdims=True)
    p = jnp.exp2(s - m0)
    m_ref[...] = m0
    l_ref[...] = jnp.sum(p, axis=0, keepdims=True)
    acc_ref[...] = jnp.dot(vt_ref[:, pl.ds(base, blk)], p.astype(BF16), preferred_element_type=F32)

    width = PAST_TILE * blk
    n_tiles = (i + PAST_TILE - 1) // PAST_TILE

    def window(t):
        return pl.ds(pl.multiple_of(jnp.minimum(t, n_blocks // PAST_TILE - 1) * width, width), width)

    def raw_scores(t):
        return _nt_dot(k_ref[0, window(t), :], q)

    def past_tile(t, raw):
        j0 = t * PAST_TILE
        sj = raw * scale2 - bias_ref[...]
        parts = []
        for u in range(PAST_TILE):
            picked = pick_ref[pl.ds(j0 + u, 1), :] > 0.5
            parts.append(jnp.where(picked, sj[u * blk:(u + 1) * blk], NEG_INF))
        sj = jnp.concatenate(parts, axis=0)
        shift = slope2 * ((i - j0) * blk).astype(F32)
        m_old = m_ref[...]
        m_new = jnp.maximum(m_old, jnp.max(sj, axis=0, keepdims=True) - shift)
        pj = jnp.exp2(sj - (m_new + shift))
        alpha = jnp.exp2(m_old - m_new)
        m_ref[...] = m_new
        l_ref[...] = alpha * l_ref[...] + jnp.sum(pj, axis=0, keepdims=True)
        acc_ref[...] = alpha * acc_ref[...] + jnp.dot(
            vt_ref[:, window(t)], pj.astype(BF16), preferred_element_type=F32)

    sa_ref[...] = raw_scores(0)

    def tile_pair(u, carry):
        sb_ref[...] = raw_scores(2 * u + 1)
        past_tile(2 * u, sa_ref[...])
        sa_ref[...] = raw_scores(2 * u + 2)
        past_tile(2 * u + 1, sb_ref[...])
        return carry

    lax.fori_loop(0, (n_tiles + 1) // 2, tile_pair, 0)
    o_ref[...] = (acc_ref[...] / l_ref[...]).T.astype(o_ref.dtype)


def moba_attention(qk, vt, batch, seq, n_heads):
    assert seq % (PAST_TILE * MOBA_BLOCK) == 0
    n = batch * seq
    nqb = seq // MOBA_BLOCK
    nbp = -(-nqb // 8) * 8
    slopes = jnp.asarray([2.0 ** (-8.0 * (h + 1) / n_heads) for h in range(n_heads)], dtype=F32)
    blk = MOBA_BLOCK
    vmem = 2 * (2 * seq * HEAD_DIM * 2) + 4 * blk * HEAD_DIM * 2 + 11 * PAST_TILE * blk * blk * 4
    grid_spec = pltpu.PrefetchScalarGridSpec(
        num_scalar_prefetch=1,
        grid=(batch, n_heads, nqb),
        in_specs=[pl.BlockSpec((1, blk, HEAD_DIM), lambda b, h, i, s: (h, b * nqb + i, 0)),
                  pl.BlockSpec((1, seq, HEAD_DIM), lambda b, h, i, s: (n_heads + h, b, 0)),
                  pl.BlockSpec((HEAD_DIM, seq), lambda b, h, i, s: (h, b))],
        out_specs=pl.BlockSpec((blk, HEAD_DIM), lambda b, h, i, s: (b * nqb + i, h)),
        scratch_shapes=[pltpu.VMEM((nbp, HEAD_DIM), F32), pltpu.VMEM((nbp, HEAD_DIM), F32),
                        pltpu.VMEM((nbp + PAST_TILE, blk), F32), pltpu.VMEM((PAST_TILE * blk, blk), F32),
                        pltpu.VMEM((1, blk), F32), pltpu.VMEM((1, blk), F32),
                        pltpu.VMEM((HEAD_DIM, blk), F32),
                        pltpu.VMEM((PAST_TILE * blk, blk), F32), pltpu.VMEM((PAST_TILE * blk, blk), F32)],
    )
    return pl.pallas_call(
        functools.partial(_moba_kernel, n_blocks=nqb),
        grid_spec=grid_spec,
        out_shape=jax.ShapeDtypeStruct((n, n_heads * HEAD_DIM), BF16),
        compiler_params=_params(("arbitrary", "arbitrary", "arbitrary"), vmem + (8 << 20)),
        name="moba",
    )(slopes, qk, qk, vt)


SB_BLOCK = 256


def _stickbreak_kernel(q_ref, k_ref, v_ref, o_ref, carry_ref, acc_ref):
    i = pl.program_id(2)
    t = SB_BLOCK
    scale = HEAD_DIM ** -0.5 * LOG2E
    q = q_ref[0]
    row = lax.broadcasted_iota(jnp.int32, (t, t), 0)
    col = lax.broadcasted_iota(jnp.int32, (t, t), 1)
    later = (row > col).astype(BF16)

    def log_sigmoid(z):
        return jnp.minimum(z, 0.0) - jnp.log2(1.0 + jnp.exp2(-jnp.abs(z)))

    def suffix_in_block(log_keep):
        hi = log_keep.astype(BF16)
        lo = (log_keep - hi.astype(F32)).astype(BF16)
        return jnp.dot(hi, later, preferred_element_type=F32) + jnp.dot(lo, later, preferred_element_type=F32)

    first = jnp.maximum(i - 1, 0)
    kbase = pl.multiple_of(first * t, t)
    z = _nt_dot(q, k_ref[0, pl.ds(kbase, 2 * t), :]) * scale
    log_beta = log_sigmoid(z)
    qrow = lax.broadcasted_iota(jnp.int32, (t, 2 * t), 0)
    kcol = lax.broadcasted_iota(jnp.int32, (t, 2 * t), 1)
    causal = kcol + (first - i) * t < qrow
    log_keep = jnp.where(causal, log_beta - z, 0.0)
    tail_sum = jnp.sum(log_keep[:, t:], axis=-1, keepdims=True)
    suffix = suffix_in_block(jnp.concatenate([log_keep[:, :t], log_keep[:, t:]], axis=0))
    suffix = jnp.concatenate([suffix[:t] + tail_sum, suffix[t:]], axis=1)
    w = jnp.where(causal, jnp.exp2(log_beta + suffix), 0.0)
    acc_ref[...] = jnp.dot(w.astype(BF16), v_ref[0, pl.ds(kbase, 2 * t), :], preferred_element_type=F32)
    carry_ref[...] = jnp.sum(log_keep[:, :t], axis=-1, keepdims=True) + tail_sum

    def key_block(jb):
        base = pl.multiple_of(jb * t, t)
        zb = _nt_dot(q, k_ref[0, pl.ds(base, t), :]) * scale
        lb = log_sigmoid(zb)
        lk = lb - zb
        wb = jnp.exp2(lb + suffix_in_block(lk) + carry_ref[...])
        acc_ref[...] += jnp.dot(wb.astype(BF16), v_ref[0, pl.ds(base, t), :], preferred_element_type=F32)
        carry_ref[...] += jnp.sum(lk, axis=-1, keepdims=True)

    def cond(state):
        jb, worst = state
        return jnp.logical_and(jb >= 0, worst > F32_EXP2_ZERO_BELOW)

    def body(state):
        jb, _ = state
        key_block(jb)
        return jb - 1, jnp.max(carry_ref[...])

    lax.while_loop(cond, body, (first - 1, jnp.max(carry_ref[...])))
    o_ref[...] = acc_ref[...].astype(o_ref.dtype)


def stickbreak_attention(kvq, batch, seq, n_heads):
    assert seq % SB_BLOCK == 0 and seq >= 2 * SB_BLOCK
    n = batch * seq
    nqb = seq // SB_BLOCK
    t = SB_BLOCK
    vmem = 2 * (2 * seq * HEAD_DIM * 2) + 4 * t * HEAD_DIM * 2 + 14 * t * t * 4
    return pl.pallas_call(
        _stickbreak_kernel,
        grid=(batch, n_heads, nqb),
        in_specs=[pl.BlockSpec((1, t, HEAD_DIM), lambda b, h, i: (2 * n_heads + h, b * nqb + i, 0)),
                  pl.BlockSpec((1, seq, HEAD_DIM), lambda b, h, i: (h, b, 0)),
                  pl.BlockSpec((1, seq, HEAD_DIM), lambda b, h, i: (n_heads + h, b, 0))],
        out_specs=pl.BlockSpec((t, HEAD_DIM), lambda b, h, i: (b * nqb + i, h)),
        out_shape=jax.ShapeDtypeStruct((n, n_heads * HEAD_DIM), BF16),
        scratch_shapes=[pltpu.VMEM((t, 1), F32), pltpu.VMEM((t, HEAD_DIM), F32)],
        compiler_params=_params(("arbitrary", "arbitrary", "arbitrary"), vmem + (8 << 20)),
        name="stickbreak",
    )(kvq, kvq, kvq)


def _pack_bf16_pairs(x):
    half = x.shape[1] // 2
    lo = lax.bitcast_convert_type(x[:, :half].astype(F32), jnp.uint32)
    hi = lax.bitcast_convert_type(x[:, half:].astype(F32), jnp.uint32)
    return (lo >> 16) | (hi & jnp.uint32(0xFFFF0000))


def _unpack_bf16_pairs(u):
    lo = lax.bitcast_convert_type(u << 16, F32)
    hi = lax.bitcast_convert_type(u & jnp.uint32(0xFFFF0000), F32)
    return jnp.concatenate([lo, hi], axis=1).astype(BF16)


def _route_kernel(x_ref, g_ref, wh_ref, wl_ref, xn_ref, r_ref, *, n_experts):
    xn = _rmsnorm(x_ref[...], g_ref[...])
    hi = xn.astype(BF16)
    xn_ref[...] = _pack_bf16_pairs(hi)
    lo = (xn - hi.astype(F32)).astype(BF16)
    wh, wl = wh_ref[...], wl_ref[...]
    logits = (jnp.dot(hi, wh, preferred_element_type=F32) + jnp.dot(hi, wl, preferred_element_type=F32)
              + jnp.dot(lo, wh, preferred_element_type=F32))
    lane = lax.broadcasted_iota(jnp.int32, logits.shape, 1)
    logits = jnp.where(lane < n_experts, logits, -jnp.inf)
    v1 = jnp.max(logits, axis=-1, keepdims=True)
    i1 = jnp.min(jnp.where(logits == v1, lane, LANES), axis=-1, keepdims=True)
    rest = jnp.where(lane == i1, -jnp.inf, logits)
    v2 = jnp.max(rest, axis=-1, keepdims=True)
    i2 = jnp.min(jnp.where(rest == v2, lane, LANES), axis=-1, keepdims=True)
    e2 = jnp.exp(v2 - v1)
    w1 = 1.0 / (1.0 + e2)
    w2 = e2 / (1.0 + e2)
    out = jnp.where(lane == 0, i1.astype(F32),
                    jnp.where(lane == 1, i2.astype(F32),
                              jnp.where(lane == 2, w1, jnp.where(lane == 3, w2, 0.0))))
    r_ref[...] = out


def route(x, g, w_router):
    n, d = x.shape
    n_experts = w_router.shape[1]
    assert n_experts <= LANES
    wpad = jnp.zeros((d, LANES), F32).at[:, :n_experts].set(w_router)
    wh = wpad.astype(BF16)
    wl = (wpad - wh.astype(F32)).astype(BF16)
    tm = _tile(n, 512)
    vmem = 4 * tm * d * 4 + 4 * tm * d * 4 + 4 * d * LANES * 2 + 4 * tm * LANES * 4
    return pl.pallas_call(
        functools.partial(_route_kernel, n_experts=n_experts),
        grid=(n // tm,),
        in_specs=[pl.BlockSpec((tm, d), lambda i: (i, 0)),
                  pl.BlockSpec((1, d), lambda i: (0, 0)),
                  pl.BlockSpec((d, LANES), lambda i: (0, 0)),
                  pl.BlockSpec((d, LANES), lambda i: (0, 0))],
        out_specs=[pl.BlockSpec((tm, d // 2), lambda i: (i, 0)),
                   pl.BlockSpec((tm, LANES), lambda i: (i, 0))],
        out_shape=[jax.ShapeDtypeStruct((n, d // 2), jnp.uint32), jax.ShapeDtypeStruct((n, LANES), F32)],
        compiler_params=_params(("arbitrary",), vmem + (8 << 20)),
        name="route",
    )(x, g.reshape(1, d), wh, wl)


def _row_copy(src_hbm, idx, dst_ref, r, sem):
    return pltpu.make_async_copy(src_hbm.at[pl.ds(idx, 1), :], dst_ref.at[pl.ds(r, 1), :], sem)


def _ring_gather(idx_ref, idx_next_ref, src_hbm, buf_ref, sem):
    s = pl.program_id(0)
    rows = buf_ref.shape[1]

    def issue(indices, slot):
        def start(r, carry):
            _row_copy(src_hbm, indices[0, 0, r], buf_ref.at[slot], r, sem.at[slot]).start()
            return carry

        lax.fori_loop(0, rows, start, 0, unroll=8)

    @pl.when(s == 0)
    def _():
        issue(idx_ref, 0)

    @pl.when(s + 1 < pl.num_programs(0))
    def _():
        issue(idx_next_ref, (s + 1) % 2)

    slot = s % 2

    def wait(r, carry):
        _row_copy(src_hbm, 0, buf_ref.at[slot], r, sem.at[slot]).wait()
        return carry

    lax.fori_loop(0, rows, wait, 0, unroll=8)
    return buf_ref.at[slot]


def _ring_specs(n_steps, rows):
    return [pl.BlockSpec((1, 1, rows), lambda i: (i, 0, 0), memory_space=pltpu.SMEM),
            pl.BlockSpec((1, 1, rows), lambda i: (jnp.minimum(i + 1, n_steps - 1), 0, 0),
                         memory_space=pltpu.SMEM)]


def _gather_rows_kernel(idx_ref, idx_next_ref, src_hbm, o_ref, buf_ref, sem):
    o_ref[...] = _unpack_bf16_pairs(_ring_gather(idx_ref, idx_next_ref, src_hbm, buf_ref, sem)[...])


def gather_rows(src, idx, rows_per_step):
    n_out = idx.shape[0]
    d = src.shape[1]
    r = rows_per_step
    assert n_out % r == 0
    n_steps = n_out // r
    idx = idx.reshape(n_steps, 1, r)
    vmem = 2 * r * d * 4 + 2 * r * d * 4 + 4 * r * d * 4
    return pl.pallas_call(
        _gather_rows_kernel,
        grid=(n_steps,),
        in_specs=_ring_specs(n_steps, r) + [pl.BlockSpec(memory_space=pl.ANY)],
        out_specs=pl.BlockSpec((r, 2 * d), lambda i: (i, 0)),
        out_shape=jax.ShapeDtypeStruct((n_out, 2 * d), BF16),
        scratch_shapes=[pltpu.VMEM((2, r, d), src.dtype), pltpu.SemaphoreType.DMA((2,))],
        compiler_params=_params(("arbitrary",), vmem + (8 << 20)),
        name="gather_rows",
    )(idx, idx, src)


def _expert_ffn_kernel(texp_ref, tvalid_ref, xs_ref, w1_ref, w3_ref, w2_ref, o_ref):
    t = pl.program_id(0)
    f = pl.program_id(1)

    @pl.when(f == 0)
    def _():
        o_ref[...] = jnp.zeros_like(o_ref)

    @pl.when(tvalid_ref[t] == 1)
    def _():
        act = _swiglu_act(xs_ref[...], w1_ref[0].astype(BF16), w3_ref[0].astype(BF16))
        o_ref[...] += jnp.dot(act, w2_ref[0].astype(BF16), preferred_element_type=F32)


def expert_ffn(xs, tile_expert, tile_valid, w1, w3, w2, tm):
    p, d = xs.shape
    dff = w1.shape[2]
    wbytes = w1.dtype.itemsize
    tf = _tile(dff, 1024 // wbytes)
    nf = dff // tf
    n_tiles = p // tm

    def wcol(t, f, te, tv):
        return (te[t], 0, jnp.where(tv[t] == 1, f, nf - 1))

    def wrow(t, f, te, tv):
        return (te[t], jnp.where(tv[t] == 1, f, nf - 1), 0)

    vmem = (2 * tm * d * 2 + 2 * tm * d * 4 + 2 * 3 * d * tf * wbytes + 3 * d * tf * 2
            + 3 * tm * tf * 4 + tm * d * 4)
    grid_spec = pltpu.PrefetchScalarGridSpec(
        num_scalar_prefetch=2,
        grid=(n_tiles, nf),
        in_specs=[pl.BlockSpec((tm, d), lambda t, f, te, tv: (t, 0)),
                  pl.BlockSpec((1, d, tf), wcol),
                  pl.BlockSpec((1, d, tf), wcol),
                  pl.BlockSpec((1, tf, d), wrow)],
        out_specs=pl.BlockSpec((tm, d), lambda t, f, te, tv: (t, 0)),
    )
    return pl.pallas_call(
        _expert_ffn_kernel,
        grid_spec=grid_spec,
        out_shape=jax.ShapeDtypeStruct((p, d), F32),
        compiler_params=_params(("arbitrary", "arbitrary"), vmem + (8 << 20)),
        name="expert_ffn",
    )(tile_expert, tile_valid, xs, w1, w3, w2)


def _combine_kernel(pos_ref, pos_next_ref, x_ref, r_ref, g_ref, ys_hbm, o_ref, buf_ref, sem):
    tc = x_ref.shape[0]
    rows = _ring_gather(pos_ref, pos_next_ref, ys_hbm, buf_ref, sem)
    h = x_ref[...]
    for k in range(MOE_TOPK):
        h = h + r_ref[:, MOE_TOPK + k:MOE_TOPK + k + 1] * rows[pl.ds(k * tc, tc), :]
    o_ref[...] = _rmsnorm(h, g_ref[...])


def combine(x, route_info, pos, ys, g):
    n, d = x.shape
    tc = pos.shape[2] // MOE_TOPK
    vmem = 4 * tc * d * 4 + 2 * MOE_TOPK * tc * d * 4 + 2 * tc * LANES * 4 + 3 * tc * d * 4
    return pl.pallas_call(
        _combine_kernel,
        grid=(n // tc,),
        in_specs=_ring_specs(n // tc, MOE_TOPK * tc) + [
                  pl.BlockSpec((tc, d), lambda i: (i, 0)),
                  pl.BlockSpec((tc, LANES), lambda i: (i, 0)),
                  pl.BlockSpec((1, d), lambda i: (0, 0)),
                  pl.BlockSpec(memory_space=pl.ANY)],
        out_specs=pl.BlockSpec((tc, d), lambda i: (i, 0)),
        out_shape=jax.ShapeDtypeStruct((n, d), F32),
        scratch_shapes=[pltpu.VMEM((2, MOE_TOPK * tc, d), F32), pltpu.SemaphoreType.DMA((2,))],
        compiler_params=_params(("arbitrary",), vmem + (8 << 20)),
        name="combine",
    )(pos, pos, x, route_info, g.reshape(1, d), ys)


def _routing_plan(experts, n_experts, tm):
    n = experts.shape[0]
    flat = experts.T.reshape(-1)
    onehot = (flat[:, None] == jnp.arange(n_experts, dtype=jnp.int32)[None, :]).astype(jnp.int32)
    csum = jnp.cumsum(onehot, axis=0)
    counts = csum[-1]
    rank = jnp.take_along_axis(csum, flat[:, None], axis=1)[:, 0] - 1
    padded = (counts + tm - 1) // tm * tm
    gend = jnp.cumsum(padded)
    pos = (gend - padded)[flat] + rank
    p_rows = MOE_TOPK * n + n_experts * tm
    tokens = jnp.tile(jnp.arange(n, dtype=jnp.int32), MOE_TOPK)
    src = jnp.zeros((p_rows,), jnp.int32).at[pos].set(tokens)
    tile_start = jnp.arange(p_rows // tm, dtype=jnp.int32) * tm
    tile_valid = (tile_start < gend[-1]).astype(jnp.int32)
    tile_expert = jnp.sum((gend[None, :] <= tile_start[:, None]).astype(jnp.int32), axis=1)
    last_expert = jnp.max(jnp.where(counts > 0, jnp.arange(n_experts, dtype=jnp.int32), 0))
    tile_expert = jnp.where(tile_valid == 1, tile_expert, last_expert)
    return src, pos.astype(jnp.int32), tile_expert, tile_valid


def moe_block(h, g_ffn, w_router, w1, w3, w2, g_final):
    n, d = h.shape
    n_experts = w_router.shape[1]
    tm = _tile(MOE_TOPK * n, 1024)
    xn, route_info = route(h, g_ffn, w_router)
    experts = route_info[:, :MOE_TOPK].astype(jnp.int32)
    src, pos, tile_expert, tile_valid = _routing_plan(experts, n_experts, tm)
    xs = gather_rows(xn, src, _tile(src.shape[0], 512))
    ys = expert_ffn(xs, tile_expert, tile_valid, w1, w3, w2, tm)
    tc = _tile(n, 256)
    pos_tiles = jnp.stack([pos[k * n:(k + 1) * n].reshape(n // tc, tc) for k in range(MOE_TOPK)], axis=1)
    return combine(h, route_info, pos_tiles.reshape(n // tc, 1, MOE_TOPK * tc), ys, g_final)


def kernel(x, attn_norm_g, ffn_norm_g, a_w_qkv, a_w_o, kv_norm_g, b_w_kv, b_w_q, b_w_o,
           dense_w1, dense_w3, dense_w2, moe_w_router, moe_w1, moe_w3, moe_w2, final_norm_g):
    assert a_w_qkv.shape[0] == 1 and b_w_q.shape[0] == 1 and dense_w1.shape[0] == 1 and moe_w1.shape[0] == 1
    batch, seq, d = x.shape
    n = batch * seq
    heads_a = a_w_qkv.shape[2] // (3 * HEAD_DIM)
    heads_b = b_w_q.shape[2] // HEAD_DIM
    bf = lambda w: w.astype(BF16)
    h = x.reshape(n, d)

    w_qk, w_v = a_w_qkv[0][:, :2 * heads_a * HEAD_DIM], a_w_qkv[0][:, 2 * heads_a * HEAD_DIM:]
    qk = norm_matmul_heads(h, attn_norm_g[0], bf(w_qk))
    vt = norm_matmul_t(h, attn_norm_g[0], bf(w_v.T))
    mix = moba_attention(qk, vt, batch, seq, heads_a)
    h = matmul_res(mix, bf(a_w_o[0]), h)
    h = dense_ffn(h, ffn_norm_g[0], bf(dense_w1[0]), bf(dense_w3[0]), bf(dense_w2[0]))

    kvq = norm2_matmul_heads(h, kv_norm_g, bf(b_w_kv), attn_norm_g[1], bf(b_w_q[0]))
    mix = stickbreak_attention(kvq, batch, seq, heads_b)
    h = matmul_res(mix, bf(b_w_o[0]), h)
    out = moe_block(h, ffn_norm_g[1], moe_w_router[0], moe_w1[0], moe_w3[0], moe_w2[0], final_norm_g)
    return out.reshape(batch, seq, d)
```
